```python
import jax, jax.numpy as jnp
from jax import lax
import numpy as np

D_MODEL = 1024
BATCH = 8
SEQ = 2048
DEPTH = 2

HEAD_DIM = 64
FOX_HEADS = 6
SB_HEADS = 6
MLA_HEADS = 4
MLA_Q_RANK = 256
MLA_KV_RANK = 128
MLA_NOPE_DIM = 64
MLA_ROPE_DIM = 32
MLA_V_DIM = 64
ROPE_THETA = 10000.0
FOX_WIDTH = FOX_HEADS * HEAD_DIM
SB_WIDTH = SB_HEADS * HEAD_DIM
MLA_WIDTH = MLA_HEADS * MLA_V_DIM
MIX_WIDTH = FOX_WIDTH + MLA_WIDTH + SB_WIDTH
Q_BLOCK = 128
IN_SPLITS = (FOX_WIDTH, FOX_WIDTH, FOX_WIDTH, FOX_HEADS,
             MLA_Q_RANK, MLA_KV_RANK, MLA_ROPE_DIM,
             SB_WIDTH, SB_WIDTH, SB_WIDTH)
IN_COLS = sum(IN_SPLITS)

PEER_HEADS = 8
PEER_N_KEYS = 128
PEER_N_EXPERTS = PEER_N_KEYS * PEER_N_KEYS
PEER_TOPK = 16
PEER_QUERY_DIM = 256
PEER_HALF = PEER_QUERY_DIM // 2
PEER_CHUNK = 128

NORM_EPS = 1e-6

kernel_name = "hybrid_fox_mla_stickbreak_peer"


def rms_norm(x, g):
    xf = x.astype(jnp.float32)
    y = xf * lax.rsqrt(jnp.mean(xf * xf, axis=-1, keepdims=True) + NORM_EPS)
    return (y * g.astype(jnp.float32)).astype(x.dtype)


def _heads(a, n):
    b, s, _ = a.shape
    return a.reshape(b, s, n, -1).transpose(0, 2, 1, 3)


def _q_block(a, i):
    return lax.dynamic_slice_in_dim(a, i * Q_BLOCK, Q_BLOCK, axis=2)


def _sweep(block_fn, seq):
    outs = lax.map(block_fn, jnp.arange(seq // Q_BLOCK))
    nb, b, h, qb, d = outs.shape
    return outs.transpose(1, 0, 3, 2, 4).reshape(b, nb * qb, h * d)


def rope(x, pos):
    half = x.shape[-1] // 2
    inv_freq = 1.0 / (ROPE_THETA ** (jnp.arange(half, dtype=jnp.float32) / half))
    ang = pos[:, None] * inv_freq[None, :]
    cos, sin = jnp.cos(ang), jnp.sin(ang)
    xf = x.astype(jnp.float32)
    x1, x2 = xf[..., :half], xf[..., half:]
    return jnp.concatenate([x1 * cos - x2 * sin, x1 * sin + x2 * cos], axis=-1).astype(x.dtype)


def forgetting_attention(q, k, v, log_f):
    seq = q.shape[2]
    scale = HEAD_DIM ** -0.5
    c = jnp.cumsum(log_f.astype(jnp.float32), axis=-1)
    s_idx = jnp.arange(seq)

    def block(i):
        qb = _q_block(q, i)
        cb = lax.dynamic_slice_in_dim(c, i * Q_BLOCK, Q_BLOCK, axis=2)
        t_idx = i * Q_BLOCK + jnp.arange(Q_BLOCK)
        logits = (jnp.einsum('bhqd,bhkd->bhqk', qb, k, preferred_element_type=jnp.float32) * scale
                  + cb[..., :, None] - c[..., None, :])
        mask = s_idx[None, :] <= t_idx[:, None]
        p = jax.nn.softmax(jnp.where(mask, logits, -jnp.inf), axis=-1)
        return jnp.einsum('bhqk,bhkd->bhqd', p.astype(v.dtype), v)

    return _sweep(block, seq)


def latent_attention(q_nope, q_rope, k_nope, k_rope, v):
    seq = q_nope.shape[2]
    scale = (MLA_NOPE_DIM + MLA_ROPE_DIM) ** -0.5
    s_idx = jnp.arange(seq)

    def block(i):
        qn, qr = _q_block(q_nope, i), _q_block(q_rope, i)
        t_idx = i * Q_BLOCK + jnp.arange(Q_BLOCK)
        logits = (jnp.einsum('bhqd,bhkd->bhqk', qn, k_nope, preferred_element_type=jnp.float32)
                  + jnp.einsum('bhqd,bkd->bhqk', qr, k_rope, preferred_element_type=jnp.float32)) * scale
        mask = s_idx[None, :] <= t_idx[:, None]
        p = jax.nn.softmax(jnp.where(mask, logits, -jnp.inf), axis=-1)
        return jnp.einsum('bhqk,bhkd->bhqd', p.astype(v.dtype), v)

    return _sweep(block, seq)


def stick_breaking_attention(q, k, v):
    seq = q.shape[2]
    scale = HEAD_DIM ** -0.5
    s_idx = jnp.arange(seq)

    def block(i):
        qb = _q_block(q, i)
        t_idx = i * Q_BLOCK + jnp.arange(Q_BLOCK)
        z = jnp.einsum('bhqd,bhkd->bhqk', qb, k, preferred_element_type=jnp.float32) * scale
        mask = s_idx[None, :] < t_idx[:, None]
        log_keep = jnp.where(mask, jax.nn.log_sigmoid(-z), 0.0)
        later = lax.cumsum(log_keep, axis=3, reverse=True) - log_keep
        weight = jnp.where(mask, jnp.exp(jax.nn.log_sigmoid(z) + later), 0.0)
        return jnp.einsum('bhqk,bhkd->bhqd', weight.astype(v.dtype), v)

    return _sweep(block, seq)


def hybrid_mixer(h, w_in, fgate_b, q_norm_g, w_uq, kv_norm_g, w_ukv,
                 out_norm_fox, out_norm_mla, out_norm_sb, w_out):
    b, s, _ = h.shape
    proj = h @ w_in
    parts, off = [], 0
    for n in IN_SPLITS:
        parts.append(proj[..., off:off + n])
        off += n
    fq, fk, fv, f_logit, c_q, c_kv, k_rope_raw, sq, sk, sv = parts
    pos = jnp.arange(s, dtype=jnp.float32)

    log_f = jax.nn.log_sigmoid((f_logit + fgate_b).astype(jnp.float32)).transpose(0, 2, 1)
    y_fox = forgetting_attention(_heads(fq, FOX_HEADS), _heads(fk, FOX_HEADS), _heads(fv, FOX_HEADS), log_f)

    q_up = _heads(rms_norm(c_q, q_norm_g) @ w_uq, MLA_HEADS)
    q_nope, q_rope = q_up[..., :MLA_NOPE_DIM], rope(q_up[..., MLA_NOPE_DIM:], pos)
    kv_up = _heads(rms_norm(c_kv, kv_norm_g) @ w_ukv, MLA_HEADS)
    k_nope, v_mla = kv_up[..., :MLA_NOPE_DIM], kv_up[..., MLA_NOPE_DIM:]
    k_rope = rope(k_rope_raw, pos)
    y_mla = latent_attention(q_nope, q_rope, k_nope, k_rope, v_mla)

    y_sb = stick_breaking_attention(_heads(sq, SB_HEADS), _heads(sk, SB_HEADS), _heads(sv, SB_HEADS))

    y = jnp.concatenate([rms_norm(y_fox.astype(h.dtype), out_norm_fox),
                         rms_norm(y_mla.astype(h.dtype), out_norm_mla),
                         rms_norm(y_sb.astype(h.dtype), out_norm_sb)], axis=-1)
    return y @ w_out


def peer_ffn(h, w_query, keys1, keys2, u, v):
    b, s, d = h.shape
    t = b * s
    xt = h.reshape(t, d)
    q = (xt @ w_query).reshape(t, PEER_HEADS, PEER_QUERY_DIM)
    s1 = jnp.einsum('thd,kd->thk', q[..., :PEER_HALF], keys1, preferred_element_type=jnp.float32)
    s2 = jnp.einsum('thd,kd->thk', q[..., PEER_HALF:], keys2, preferred_element_type=jnp.float32)
    top1, idx1 = lax.top_k(s1, PEER_TOPK)
    top2, idx2 = lax.top_k(s2, PEER_TOPK)
    cand_score = (top1[..., :, None] + top2[..., None, :]).reshape(t, PEER_HEADS, PEER_TOPK * PEER_TOPK)
    cand_id = (idx1[..., :, None] * PEER_N_KEYS + idx2[..., None, :]).reshape(t, PEER_HEADS, PEER_TOPK * PEER_TOPK)
    best, pos = lax.top_k(cand_score, PEER_TOPK)
    expert_id = jnp.take_along_axis(cand_id, pos, axis=-1)
    gates = jax.nn.softmax(best, axis=-1)

    n_chunks = t // PEER_CHUNK
    xc = xt.reshape(n_chunks, PEER_CHUNK, d)
    idc = expert_id.reshape(n_chunks, PEER_CHUNK, PEER_HEADS, PEER_TOPK)
    gc = gates.reshape(n_chunks, PEER_CHUNK, PEER_HEADS, PEER_TOPK)

    def chunk(args):
        xi, ids, g = args
        act = jnp.einsum('cd,chkd->chk', xi, u[ids], preferred_element_type=jnp.float32)
        a = jax.nn.gelu(act, approximate=False) * g
        return jnp.einsum('chk,chkd->cd', a.astype(v.dtype), v[ids])

    return lax.map(chunk, (xc, idc, gc)).reshape(b, s, d)


def setup_inputs(seed: int = 0) -> dict:
    key = jax.random.key(seed)
    ks = jax.random.split(key, 20)
    f32 = jnp.float32

    def nrm(k, shape, scale):
        return jax.random.normal(k, shape, f32) * scale

    def gain(k, shape):
        return 1.0 + 0.02 * jax.random.normal(k, shape, f32)

    return {
        "x": nrm(ks[0], (BATCH, SEQ, D_MODEL), 1.0),
        "attn_norm_g": gain(ks[1], (DEPTH, D_MODEL)),
        "w_in": nrm(ks[2], (DEPTH, D_MODEL, IN_COLS), D_MODEL ** -0.5),
        "fgate_b": 3.0 + 0.1 * jax.random.normal(ks[3], (DEPTH, FOX_HEADS), f32),
        "mla_q_norm_g": gain(ks[4], (DEPTH, MLA_Q_RANK)),
        "w_uq": nrm(ks[5], (DEPTH, MLA_Q_RANK, MLA_HEADS * (MLA_NOPE_DIM + MLA_ROPE_DIM)), MLA_Q_RANK ** -0.5),
        "mla_kv_norm_g": gain(ks[6], (DEPTH, MLA_KV_RANK)),
        "w_ukv": nrm(ks[7], (DEPTH, MLA_KV_RANK, MLA_HEADS * (MLA_NOPE_DIM + MLA_V_DIM)), MLA_KV_RANK ** -0.5),
        "out_norm_fox": gain(ks[8], (DEPTH, FOX_WIDTH)),
        "out_norm_mla": gain(ks[9], (DEPTH, MLA_WIDTH)),
        "out_norm_sb": gain(ks[10], (DEPTH, SB_WIDTH)),
        "w_out": nrm(ks[11], (DEPTH, MIX_WIDTH, D_MODEL), MIX_WIDTH ** -0.5),
        "ffn_norm_g": gain(ks[12], (DEPTH, D_MODEL)),
        "peer_w_query": nrm(ks[13], (DEPTH, D_MODEL, PEER_HEADS * PEER_QUERY_DIM), D_MODEL ** -0.5),
        "peer_keys1": nrm(ks[14], (DEPTH, PEER_N_KEYS, PEER_HALF), PEER_HALF ** -0.5),
        "peer_keys2": nrm(ks[15], (DEPTH, PEER_N_KEYS, PEER_HALF), PEER_HALF ** -0.5),
        "peer_u": nrm(ks[16], (DEPTH, PEER_N_EXPERTS, D_MODEL), D_MODEL ** -0.5),
        "peer_v": nrm(ks[17], (DEPTH, PEER_N_EXPERTS, D_MODEL), PEER_HEADS ** -0.5),
        "final_norm_g": gain(ks[18], (D_MODEL,)),
    }


def reference(x, attn_norm_g, w_in, fgate_b, mla_q_norm_g, w_uq, mla_kv_norm_g, w_ukv,
              out_norm_fox, out_norm_mla, out_norm_sb, w_out, ffn_norm_g, peer_w_query,
              peer_keys1, peer_keys2, peer_u, peer_v, final_norm_g):
    for l in range(DEPTH):
        h = rms_norm(x, attn_norm_g[l])
        x = x + hybrid_mixer(h, w_in[l], fgate_b[l], mla_q_norm_g[l], w_uq[l], mla_kv_norm_g[l],
                             w_ukv[l], out_norm_fox[l], out_norm_mla[l], out_norm_sb[l], w_out[l])
        h = rms_norm(x, ffn_norm_g[l])
        x = x + peer_ffn(h, peer_w_query[l], peer_keys1[l], peer_keys2[l], peer_u[l], peer_v[l])
    return rms_norm(x, final_norm_g)
```

```python
import functools

import jax
import jax.numpy as jnp
import numpy as np
from jax import lax
from jax.experimental import pallas as pl
from jax.experimental.pallas import tpu as pltpu

F32, BF16, I32 = jnp.float32, jnp.bfloat16, jnp.int32

NORM_EPS = 1e-6
HEAD_DIM = 64
FOX_HEADS = 6
SB_HEADS = 6
MLA_HEADS = 4
MLA_Q_RANK = 256
MLA_KV_RANK = 128
MLA_NOPE_DIM = 64
MLA_ROPE_DIM = 32
MLA_V_DIM = 64
ROPE_THETA = 10000.0
PEER_HEADS = 8
PEER_N_KEYS = 128
PEER_TOPK = 16
PEER_HALF = 128

LANES = 128
MIB = 1024 * 1024
GATE_PITCH = 136


def _cparams(sem, vmem_mib):
    return pltpu.CompilerParams(dimension_semantics=sem, vmem_limit_bytes=vmem_mib * MIB)


def _rms(x, g):
    return x * lax.rsqrt(jnp.mean(x * x, axis=-1, keepdims=True) + NORM_EPS) * g


def _log_sigmoid(y):
    return jnp.minimum(y, 0.0) - jnp.log1p(jnp.exp(-jnp.abs(y)))


def _dot(a, b):
    return jnp.dot(a, b, preferred_element_type=F32)


def _dot_nt(a, b):
    return lax.dot_general(a, b, (((1,), (1,)), ((), ())), preferred_element_type=F32)


def _dot_tn(a, b):
    return lax.dot_general(a, b, (((0,), (0,)), ((), ())), preferred_element_type=F32)


_W384 = FOX_HEADS * HEAD_DIM
_OFF_FQ, _OFF_FK, _OFF_FV = 0, 384, 768
_OFF_SQ, _OFF_SK, _OFF_SV = 1152, 1536, 1920
_OFF_CQ, _OFF_CKV, _OFF_FLOG, _OFF_KR = 2304, 2560, 2688, 2816
_IN_COLS_PADDED = 2944


def _inproj_kernel(seq, x_ref, g_ref, w_ref, fb_ref, qg_ref, wuq_ref, kvg_ref, wukv_ref, freq_ref,
                   fq_ref, fk_ref, fv_ref, sq_ref, sk_ref, sv_ref, logf_ref, mq_ref, mk_ref, mv_ref):
    ts = x_ref.shape[0]
    h = _rms(x_ref[...], g_ref[...]).astype(BF16)
    p = _dot(h, w_ref[...])
    qk_scale = HEAD_DIM ** -0.5
    fq_ref[...] = (p[:, _OFF_FQ:_OFF_FQ + 384] * qk_scale).astype(BF16)
    fk_ref[...] = p[:, _OFF_FK:_OFF_FK + 384].astype(BF16)
    fv_ref[...] = p[:, _OFF_FV:_OFF_FV + 384].astype(BF16)
    sq_ref[...] = (p[:, _OFF_SQ:_OFF_SQ + 384] * qk_scale).astype(BF16)
    sk_ref[...] = p[:, _OFF_SK:_OFF_SK + 384].astype(BF16)
    sv_ref[...] = p[:, _OFF_SV:_OFF_SV + 384].astype(BF16)
    logf_ref[...] = _log_sigmoid(p[:, _OFF_FLOG:_OFF_FLOG + LANES] + fb_ref[...])

    mla_scale = (MLA_NOPE_DIM + MLA_ROPE_DIM) ** -0.5
    cq = _rms(p[:, _OFF_CQ:_OFF_CQ + MLA_Q_RANK], qg_ref[...]).astype(BF16)
    qup = _dot(cq, wuq_ref[...])
    ckv = _rms(p[:, _OFF_CKV:_OFF_CKV + MLA_KV_RANK], kvg_ref[...]).astype(BF16)
    kvup = _dot(ckv, wukv_ref[...])

    pos0 = lax.rem(pl.program_id(0) * ts, seq)
    pos = (pos0 + lax.broadcasted_iota(I32, (ts, LANES), 0)).astype(F32)
    ang = pos * freq_ref[...]
    cos, sin = jnp.cos(ang), jnp.sin(ang)
    lane = lax.broadcasted_iota(I32, (ts, LANES), 1)
    sin_signed = jnp.where(lane < LANES // 2, -sin, sin)

    def rope(x):
        return x * cos + pltpu.roll(x, LANES // 2, 1) * sin_signed

    qr = (rope(qup[:, 256:384]) * mla_scale).astype(BF16)
    kr = rope(p[:, _OFF_KR:_OFF_KR + LANES]).astype(BF16)
    mq_ref[:, 0:128] = (qup[:, 0:128] * mla_scale).astype(BF16)
    mq_ref[:, 128:256] = qr
    mq_ref[:, 256:384] = (qup[:, 128:256] * mla_scale).astype(BF16)
    mq_ref[:, 384:512] = qr
    mk_ref[:, 0:128] = kvup[:, 0:128].astype(BF16)
    mk_ref[:, 128:256] = kr
    mk_ref[:, 256:384] = kvup[:, 128:256].astype(BF16)
    mk_ref[:, 384:512] = kr
    mv_ref[...] = kvup[:, 256:512].astype(BF16)


def _inproj(x, g, w_all, fb, qg, wuq, kvg, wukv, freq, *, seq, ts):
    t, d = x.shape
    full = lambda a: pl.BlockSpec(a.shape, lambda i: (0,) * a.ndim)
    row = lambda w: pl.BlockSpec((ts, w), lambda i: (i, 0))
    outs = ([jax.ShapeDtypeStruct((t, 384), BF16)] * 6 + [jax.ShapeDtypeStruct((t, LANES), F32)]
            + [jax.ShapeDtypeStruct((t, 512), BF16)] * 2 + [jax.ShapeDtypeStruct((t, 256), BF16)])
    return pl.pallas_call(
        functools.partial(_inproj_kernel, seq),
        grid=(t // ts,),
        in_specs=[row(d), full(g), full(w_all), full(fb), full(qg), full(wuq), full(kvg), full(wukv), full(freq)],
        out_specs=[row(384)] * 6 + [row(LANES)] + [row(512)] * 2 + [row(256)],
        out_shape=outs,
        compiler_params=_cparams(("arbitrary",), 52),
        name="inproj",
    )(x, g, w_all, fb, qg, wuq, kvg, wukv, freq)


def _split3(x):
    hi = x.astype(BF16)
    r = x - hi.astype(F32)
    mid = r.astype(BF16)
    lo = (r - mid.astype(F32)).astype(BF16)
    return hi, mid, lo


def _cumsum_kernel(logf_ref, c_ref):
    seq = logf_ref.shape[1]
    r = lax.broadcasted_iota(I32, (LANES, LANES), 0)
    c = lax.broadcasted_iota(I32, (LANES, LANES), 1)
    tri = jnp.where(r <= c, 1.0, 0.0).astype(BF16)
    carry = jnp.zeros((LANES, 1), F32)
    for i in range(seq // LANES):
        chunk = logf_ref[0, i * LANES:(i + 1) * LANES, :]
        hi, mid, lo = _split3(chunk)
        cs = _dot_tn(hi, tri) + _dot_tn(mid, tri) + _dot_tn(lo, tri) + carry
        c_ref[0, :, i * LANES:(i + 1) * LANES] = cs[0:8, :]
        carry = cs[:, LANES - 1:LANES]


def _fox_cumsum(logf):
    b, s, _ = logf.shape
    return pl.pallas_call(
        _cumsum_kernel,
        grid=(b,),
        in_specs=[pl.BlockSpec((1, s, LANES), lambda i: (i, 0, 0))],
        out_specs=pl.BlockSpec((1, 8, s), lambda i: (i, 0, 0)),
        out_shape=jax.ShapeDtypeStruct((b, 8, s), F32),
        compiler_params=_cparams(("arbitrary",), 16),
        name="fox_cumsum",
    )(logf)


def _head_masks(kind, wq, pair):
    lane = lax.broadcasted_iota(I32, (1, wq), 1)
    masks = []
    for e in range(2):
        m = (lane >= HEAD_DIM * e) & (lane < HEAD_DIM * (e + 1))
        if kind == "mla":
            head = 2 * pair + e
            rl = jnp.bitwise_and(lane - LANES, LANES // 2 - 1)
            m = m | ((lane >= LANES) & (rl >= 16 * head) & (rl < 16 * (head + 1)))
        masks.append(m)
    return masks


def _softmax_attn_kernel(kind, bk, *refs):
    if kind == "fox":
        q_ref, k_ref, v_ref, c_ref, o_ref = refs
    else:
        q_ref, k_ref, v_ref, o_ref = refs
    tq, wq = q_ref.shape[1], q_ref.shape[2]
    pair, qi = pl.program_id(1), pl.program_id(2)
    q2 = q_ref[0]
    zero = jnp.zeros_like(q2)
    qs = [jnp.where(m, q2, zero) for m in _head_masks(kind, wq, pair)]
    row = qi * tq + lax.broadcasted_iota(I32, (tq, 1), 0)
    nkb = lax.div((qi + 1) * tq + bk - 1, bk)

    def body(kb, carry):
        k0 = pl.multiple_of(kb * bk, bk)
        k2 = k_ref[0, pl.ds(k0, bk), :]
        v2 = v_ref[0, pl.ds(k0, bk), :]
        valid = (k0 + lax.broadcasted_iota(I32, (1, bk), 1)) <= row
        out = []
        for e in range(2):
            m, l, acc = carry[3 * e:3 * e + 3]
            s = _dot_nt(qs[e], k2)
            if kind == "fox":
                s = s - c_ref[0, pl.ds(2 * pair + e, 1), pl.ds(k0, bk)]
            s = jnp.where(valid, s, -jnp.inf)
            m_new = jnp.maximum(m, jnp.max(s, axis=1, keepdims=True))
            alpha = jnp.exp(m - m_new)
            p = jnp.exp(s - m_new)
            l = alpha * l + jnp.sum(p, axis=1, keepdims=True)
            acc = alpha * acc + _dot(p.astype(BF16), v2)
            out += [m_new, l, acc]
        return tuple(out)

    init = (jnp.full((tq, 1), -jnp.inf, F32), jnp.zeros((tq, 1), F32), jnp.zeros((tq, LANES), F32)) * 2
    res = lax.fori_loop(0, nkb, body, init)
    lane = lax.broadcasted_iota(I32, (tq, LANES), 1)
    o_ref[0] = jnp.where(lane < HEAD_DIM, res[2] / res[1], res[5] / res[4])


def _sb_attn_kernel(bk, q_ref, k_ref, v_ref, o_ref):
    tq, wq = q_ref.shape[1], q_ref.shape[2]
    pair, qi = pl.program_id(1), pl.program_id(2)
    q2 = q_ref[0]
    zero = jnp.zeros_like(q2)
    qs = [jnp.where(m, q2, zero) for m in _head_masks("sb", wq, pair)]
    row = qi * tq + lax.broadcasted_iota(I32, (tq, 1), 0)
    nkb = lax.div((qi + 1) * tq + bk - 1, bk)
    jj = lax.broadcasted_iota(I32, (bk, bk), 0)
    ss = lax.broadcasted_iota(I32, (bk, bk), 1)
    later_mat = jnp.where(jj > ss, 1.0, 0.0).astype(BF16)

    def body(it, carry):
        kb = nkb - 1 - it
        k0 = pl.multiple_of(kb * bk, bk)
        k2 = k_ref[0, pl.ds(k0, bk), :]
        v2 = v_ref[0, pl.ds(k0, bk), :]
        valid = (k0 + lax.broadcasted_iota(I32, (1, bk), 1)) < row
        out = []
        for e in range(2):
            run, acc = carry[2 * e:2 * e + 2]
            z = _dot_nt(qs[e], k2)
            sp = jnp.maximum(z, 0.0) + jnp.log1p(jnp.exp(-jnp.abs(z)))
            log_keep = jnp.where(valid, -sp, 0.0)
            hi = log_keep.astype(BF16)
            lo = (log_keep - hi.astype(F32)).astype(BF16)
            later = _dot(hi, later_mat) + _dot(lo, later_mat) + run
            w = jnp.where(valid, jnp.exp(z - sp + later), 0.0)
            acc = acc + _dot(w.astype(BF16), v2)
            run = run + jnp.sum(log_keep, axis=1, keepdims=True)
            out += [run, acc]
        return tuple(out)

    init = (jnp.zeros((tq, 1), F32), jnp.zeros((tq, LANES), F32)) * 2
    res = lax.fori_loop(0, nkb, body, init)
    lane = lax.broadcasted_iota(I32, (tq, LANES), 1)
    o_ref[0] = jnp.where(lane < HEAD_DIM, res[1], res[3])


def _pair_attention(kind, q, k, v, c=None, *, tq, bk):
    b, s, wtot = q.shape
    npairs = v.shape[2] // LANES
    wq = wtot // npairs
    in_specs = [pl.BlockSpec((1, tq, wq), lambda bb, p, i: (bb, i, p)),
                pl.BlockSpec((1, s, wq), lambda bb, p, i: (bb, 0, p)),
                pl.BlockSpec((1, s, LANES), lambda bb, p, i: (bb, 0, p))]
    args = [q, k, v]
    if kind == "fox":
        in_specs.append(pl.BlockSpec((1, 8, s), lambda bb, p, i: (bb, 0, 0)))
        args.append(c)
    body = functools.partial(_sb_attn_kernel, bk) if kind == "sb" else functools.partial(_softmax_attn_kernel, kind, bk)
    return pl.pallas_call(
        body,
        grid=(b, npairs, s // tq),
        in_specs=in_specs,
        out_specs=pl.BlockSpec((1, tq, LANES), lambda bb, p, i: (bb, i, p)),
        out_shape=jax.ShapeDtypeStruct((b, s, npairs * LANES), F32),
        compiler_params=_cparams(("arbitrary",) * 3, 40),
        name=kind + "_attention",
    )(*args)


def _outproj_kernel(x_ref, yf_ref, ym_ref, ys_ref, gf_ref, gm_ref, gs_ref, wf_ref, wm_ref, ws_ref, o_ref):
    y = _dot(_rms(yf_ref[...], gf_ref[...]).astype(BF16), wf_ref[...])
    y += _dot(_rms(ym_ref[...], gm_ref[...]).astype(BF16), wm_ref[...])
    y += _dot(_rms(ys_ref[...], gs_ref[...]).astype(BF16), ws_ref[...])
    o_ref[...] = x_ref[...] + y


def _outproj(x, yf, ym, ys, gf, gm, gs, wf, wm, ws, *, ts):
    t, d = x.shape
    full = lambda a: pl.BlockSpec(a.shape, lambda i: (0,) * a.ndim)
    row = lambda a: pl.BlockSpec((ts, a.shape[1]), lambda i: (i, 0))
    return pl.pallas_call(
        _outproj_kernel,
        grid=(t // ts,),
        in_specs=[row(x), row(yf), row(ym), row(ys), full(gf), full(gm), full(gs), full(wf), full(wm), full(ws)],
        out_specs=row(x),
        out_shape=jax.ShapeDtypeStruct((t, d), F32),
        compiler_params=_cparams(("arbitrary",), 40),
        name="outproj",
    )(x, yf, ym, ys, gf, gm, gs, wf, wm, ws)


def _extract_topk(work_ref, n_rows, val_ref, idx_ref):
    rows = lax.broadcasted_iota(I32, (n_rows, work_ref.shape[1]), 0)

    def body(i, c):
        s = work_ref[0:n_rows, :]
        m = jnp.max(s, axis=0, keepdims=True)
        idx = jnp.min(jnp.where(s == m, rows, n_rows), axis=0, keepdims=True)
        work_ref[0:n_rows, :] = jnp.where(rows == idx, -jnp.inf, s)
        val_ref[pl.ds(i, 1), :] = m
        idx_ref[pl.ds(i, 1), :] = idx
        return c

    lax.fori_loop(0, PEER_TOPK, body, 0)


def _route_kernel(x_ref, g_ref, wq_ref, k1_ref, k2_ref, h_ref, ids_ref, gates_ref,
                  q_s, work, top_s, idx_s, best_v, best_p, ids_t, gates_t):
    k = PEER_TOPK
    hb = _rms(x_ref[...], g_ref[...]).astype(BF16)
    h_ref[...] = hb
    q = _dot(hb, wq_ref[...]).astype(BF16)
    for i in range(2 * PEER_HEADS):
        q_s[i] = q[:, i * PEER_HALF:(i + 1) * PEER_HALF]

    def head(h, c):
        for half, key_ref in enumerate((k1_ref, k2_ref)):
            work[0:PEER_N_KEYS, :] = _dot_nt(key_ref[...], q_s[2 * h + half])
            _extract_topk(work, PEER_N_KEYS, top_s.at[half], idx_s.at[half])
        top2 = top_s[1]
        for j in range(k):
            work[j * k:(j + 1) * k, :] = top_s[0, j:j + 1, :] + top2
        _extract_topk(work, k * k, best_v, best_p)
        pos = best_p[...]
        pj, pk = lax.shift_right_logical(pos, 4), jnp.bitwise_and(pos, k - 1)
        a = jnp.zeros_like(pos)
        b = jnp.zeros_like(pos)
        for j in range(k):
            a = jnp.where(pj == j, idx_s[0, j:j + 1, :], a)
            b = jnp.where(pk == j, idx_s[1, j:j + 1, :], b)
        best = best_v[...]
        e = jnp.exp(best - best[0:1, :])
        r0 = pl.multiple_of(h * k, k)
        ids_t[pl.ds(r0, k), :] = (a * PEER_N_KEYS + b).astype(F32)
        gates_t[pl.ds(r0, k), :] = e / jnp.sum(e, axis=0, keepdims=True)
        return c

    lax.fori_loop(0, PEER_HEADS, head, 0)
    ids_ref[...] = ids_t[...].T.astype(I32)
    gates_ref[...] = gates_t[...].T


def _peer_route(x, g, wq, k1, k2, *, tr):
    t, d = x.shape
    full = lambda a: pl.BlockSpec(a.shape, lambda i: (0,) * a.ndim)
    row = lambda w: pl.BlockSpec((tr, w), lambda i: (i, 0))
    k = PEER_TOPK
    return pl.pallas_call(
        _route_kernel,
        grid=(t // tr,),
        in_specs=[row(d), full(g), full(wq), full(k1), full(k2)],
        out_specs=[row(d), row(LANES), row(LANES)],
        out_shape=[jax.ShapeDtypeStruct((t, d), BF16), jax.ShapeDtypeStruct((t, LANES), I32),
                   jax.ShapeDtypeStruct((t, LANES), F32)],
        scratch_shapes=[pltpu.VMEM((2 * PEER_HEADS, tr, PEER_HALF), BF16), pltpu.VMEM((k * k, tr), F32),
                        pltpu.VMEM((2, k, tr), F32), pltpu.VMEM((2, k, tr), I32),
                        pltpu.VMEM((k, tr), F32), pltpu.VMEM((k, tr), I32),
                        pltpu.VMEM((PEER_HEADS * k, tr), F32), pltpu.VMEM((PEER_HEADS * k, tr), F32)],
        compiler_params=_cparams(("arbitrary",), 40),
        name="peer_route",
    )(x, g, wq, k1, k2)


def _gelu(x):
    return 0.5 * x * (1.0 + lax.erf(x * np.float32(np.sqrt(0.5))))


def _peer_kernel(final_norm, h_ref, ids_ref, gates_ref, u_ref, v_ref, x_ref, fg_ref, o_ref, gate_s, a_s, acc_s):
    tq = h_ref.shape[0]
    nb = u_ref.shape[0]
    j = pl.program_id(1)

    @pl.when(j == 0)
    def _():
        key = lax.broadcasted_iota(I32, (PEER_N_KEYS, LANES), 0)

        def build(t, c):
            ids = ids_ref[pl.ds(t, 1), :]
            gate = gates_ref[pl.ds(t, 1), :]
            i1 = lax.shift_right_logical(ids, 7)
            i2 = jnp.bitwise_and(ids, PEER_N_KEYS - 1)
            p = jnp.where(i1 == key, gate, 0.0).astype(BF16)
            q = jnp.where(i2 == key, 1.0, 0.0).astype(BF16)
            gate_s[pl.ds(pl.multiple_of(t * GATE_PITCH, 8), PEER_N_KEYS), :] = _dot_nt(p, q)
            return c

        lax.fori_loop(0, tq, build, 0)
        acc_s[...] = jnp.zeros_like(acc_s)

    act = _dot_nt(h_ref[...], u_ref[...])
    rows_per_block = nb // PEER_N_KEYS
    for r in range(rows_per_block):
        g = gate_s[pl.ds(j * rows_per_block + r, tq, stride=GATE_PITCH), :]
        a_s[:, r * LANES:(r + 1) * LANES] = (_gelu(act[:, r * LANES:(r + 1) * LANES]) * g).astype(BF16)
    acc_s[...] += _dot(a_s[...], v_ref[...])

    @pl.when(j == pl.num_programs(1) - 1)
    def _():
        y = x_ref[...] + acc_s[...]
        o_ref[...] = _rms(y, fg_ref[...]) if final_norm else y


def _peer_main(h, ids, gates, u, v, x, fg, *, final_norm, tq, nb):
    t, d = x.shape
    n = u.shape[0]
    tok = lambda w: pl.BlockSpec((tq, w), lambda i, j: (i, 0))
    return pl.pallas_call(
        functools.partial(_peer_kernel, final_norm),
        grid=(t // tq, n // nb),
        in_specs=[tok(d), tok(LANES), tok(LANES),
                  pl.BlockSpec((nb, d), lambda i, j: (j, 0)), pl.BlockSpec((nb, d), lambda i, j: (j, 0)),
                  tok(d), pl.BlockSpec((1, d), lambda i, j: (0, 0))],
        out_specs=tok(d),
        out_shape=jax.ShapeDtypeStruct((t, d), F32),
        scratch_shapes=[pltpu.VMEM((tq * GATE_PITCH, LANES), F32), pltpu.VMEM((tq, nb), BF16),
                        pltpu.VMEM((tq, d), F32)],
        compiler_params=_cparams(("arbitrary", "arbitrary"), 52),
        name="peer_main",
    )(h, ids, gates, u, v, x, fg)


def _prep_w_in(w_in, fgate_b):
    d = w_in.shape[0]
    o = np.cumsum([0, 384, 384, 384, FOX_HEADS, MLA_Q_RANK, MLA_KV_RANK, MLA_ROPE_DIM, 384, 384, 384])
    fq, fk, fv, fl, cq, ckv, kr, sq, sk, sv = [w_in[:, o[i]:o[i + 1]] for i in range(10)]
    half = MLA_ROPE_DIM // 2
    kr4 = jnp.concatenate([kr[:, :half]] * MLA_HEADS + [kr[:, half:]] * MLA_HEADS, axis=1)
    fl_pad = jnp.pad(fl, ((0, 0), (0, LANES - FOX_HEADS)))
    w_all = jnp.concatenate([fq, fk, fv, sq, sk, sv, cq, ckv, fl_pad, kr4], axis=1).astype(BF16)
    assert w_all.shape == (d, _IN_COLS_PADDED)
    fb = jnp.pad(fgate_b, (0, LANES - FOX_HEADS)).reshape(1, LANES)
    return w_all, fb


def _prep_w_uq(w_uq):
    per = MLA_NOPE_DIM + MLA_ROPE_DIM
    half = MLA_ROPE_DIM // 2
    nope = [w_uq[:, h * per:h * per + MLA_NOPE_DIM] for h in range(MLA_HEADS)]
    x1 = [w_uq[:, h * per + MLA_NOPE_DIM:h * per + MLA_NOPE_DIM + half] for h in range(MLA_HEADS)]
    x2 = [w_uq[:, h * per + MLA_NOPE_DIM + half:(h + 1) * per] for h in range(MLA_HEADS)]
    return jnp.concatenate(nope + x1 + x2, axis=1).astype(BF16)


def _prep_w_ukv(w_ukv):
    per = MLA_NOPE_DIM + MLA_V_DIM
    kn = [w_ukv[:, h * per:h * per + MLA_NOPE_DIM] for h in range(MLA_HEADS)]
    vv = [w_ukv[:, h * per + MLA_NOPE_DIM:(h + 1) * per] for h in range(MLA_HEADS)]
    return jnp.concatenate(kn + vv, axis=1).astype(BF16)


def _rope_lane_freq():
    half = MLA_ROPE_DIM // 2
    inv_freq = 1.0 / (ROPE_THETA ** (jnp.arange(half, dtype=F32) / half))
    return jnp.tile(inv_freq, LANES // half).reshape(1, LANES)


def kernel(x, attn_norm_g, w_in, fgate_b, mla_q_norm_g, w_uq, mla_kv_norm_g, w_ukv, out_norm_fox, out_norm_mla, out_norm_sb, w_out, ffn_norm_g, peer_w_query, peer_keys1, peer_keys2, peer_u, peer_v, final_norm_g):
    b, s, d = x.shape
    depth = w_in.shape[0]
    t = b * s
    xt = x.reshape(t, d)
    freq = _rope_lane_freq()
    row = lambda a: a.reshape(1, -1)
    for l in range(depth):
        w_all, fb = _prep_w_in(w_in[l], fgate_b[l])
        fq, fk, fv, sq, sk, sv, logf, mq, mk, mv = _inproj(
            xt, row(attn_norm_g[l]), w_all, fb, row(mla_q_norm_g[l]), _prep_w_uq(w_uq[l]),
            row(mla_kv_norm_g[l]), _prep_w_ukv(w_ukv[l]), freq, seq=s, ts=512)
        bs = lambda a: a.reshape(b, s, a.shape[1])
        c = _fox_cumsum(bs(logf))
        y_fox = _pair_attention("fox", bs(fq), bs(fk), bs(fv), c, tq=256, bk=512)
        y_mla = _pair_attention("mla", bs(mq), bs(mk), bs(mv), tq=256, bk=512)
        y_sb = _pair_attention("sb", bs(sq), bs(sk), bs(sv), tq=256, bk=256)
        wo = w_out[l].astype(BF16)
        xt = _outproj(xt, y_fox.reshape(t, -1), y_mla.reshape(t, -1), y_sb.reshape(t, -1),
                      row(out_norm_fox[l]), row(out_norm_mla[l]), row(out_norm_sb[l]),
                      wo[0:384], wo[384:640], wo[640:1024], ts=512)
        h2, ids, gates = _peer_route(xt, row(ffn_norm_g[l]), peer_w_query[l].astype(BF16),
                                     peer_keys1[l].astype(BF16), peer_keys2[l].astype(BF16), tr=256)
        xt = _peer_main(h2, ids, gates, peer_u[l].astype(BF16), peer_v[l].astype(BF16), xt,
                        row(final_norm_g), final_norm=(l == depth - 1), tq=256, nb=1024)
    return xt.reshape(b, s, d)
```

```python
import functools

import jax
import jax.numpy as jnp
import numpy as np
from jax import lax
from jax.experimental import pallas as pl
from jax.experimental.pallas import tpu as pltpu

F32, BF16, I32 = jnp.float32, jnp.bfloat16, jnp.int32

NORM_EPS = 1e-6
HEAD_DIM = 64
FOX_HEADS = 6
SB_HEADS = 6
MLA_HEADS = 4
MLA_Q_RANK = 256
MLA_KV_RANK = 128
MLA_NOPE_DIM = 64
MLA_ROPE_DIM = 32
MLA_V_DIM = 64
ROPE_THETA = 10000.0
PEER_HEADS = 8
PEER_N_KEYS = 128
PEER_TOPK = 16
PEER_HALF = 128

LANES = 128
MIB = 1024 * 1024
GATE_PITCH = 72
GATE_UNROLL = 16
PEER_CHUNK = 256


def _cparams(sem, vmem_mib):
    return pltpu.CompilerParams(dimension_semantics=sem, vmem_limit_bytes=vmem_mib * MIB)


def _rms(x, g):
    return x * lax.rsqrt(jnp.mean(x * x, axis=-1, keepdims=True) + NORM_EPS) * g


def _log_sigmoid(y):
    return jnp.minimum(y, 0.0) - jnp.log1p(jnp.exp(-jnp.abs(y)))


def _dot(a, b):
    return jnp.dot(a, b, preferred_element_type=F32)


def _dot_nt(a, b):
    return lax.dot_general(a, b, (((1,), (1,)), ((), ())), preferred_element_type=F32)


def _dot_tn(a, b):
    return lax.dot_general(a, b, (((0,), (0,)), ((), ())), preferred_element_type=F32)


_W384 = FOX_HEADS * HEAD_DIM
_OFF_FQ, _OFF_FK, _OFF_FV = 0, 384, 768
_OFF_SQ, _OFF_SK, _OFF_SV = 1152, 1536, 1920
_OFF_CQ, _OFF_CKV, _OFF_FLOG, _OFF_KR = 2304, 2560, 2688, 2816
_IN_COLS_PADDED = 2944


def _inproj_kernel(seq, x_ref, g_ref, w_ref, fb_ref, qg_ref, wuq_ref, kvg_ref, wukv_ref, freq_ref,
                   fq_ref, fk_ref, fv_ref, sq_ref, sk_ref, sv_ref, logf_ref, mq_ref, mk_ref, mv_ref):
    ts = x_ref.shape[0]
    h = _rms(x_ref[...], g_ref[...]).astype(BF16)
    p = _dot(h, w_ref[...])
    qk_scale = HEAD_DIM ** -0.5
    fq_ref[...] = (p[:, _OFF_FQ:_OFF_FQ + 384] * qk_scale).astype(BF16)
    fk_ref[...] = p[:, _OFF_FK:_OFF_FK + 384].astype(BF16)
    fv_ref[...] = p[:, _OFF_FV:_OFF_FV + 384].astype(BF16)
    sq_ref[...] = (p[:, _OFF_SQ:_OFF_SQ + 384] * qk_scale).astype(BF16)
    sk_ref[...] = p[:, _OFF_SK:_OFF_SK + 384].astype(BF16)
    sv_ref[...] = p[:, _OFF_SV:_OFF_SV + 384].astype(BF16)
    logf_ref[...] = _log_sigmoid(p[:, _OFF_FLOG:_OFF_FLOG + LANES] + fb_ref[...])

    mla_scale = (MLA_NOPE_DIM + MLA_ROPE_DIM) ** -0.5
    cq = _rms(p[:, _OFF_CQ:_OFF_CQ + MLA_Q_RANK], qg_ref[...]).astype(BF16)
    qup = _dot(cq, wuq_ref[...])
    ckv = _rms(p[:, _OFF_CKV:_OFF_CKV + MLA_KV_RANK], kvg_ref[...]).astype(BF16)
    kvup = _dot(ckv, wukv_ref[...])

    pos0 = lax.rem(pl.program_id(0) * ts, seq)
    pos = (pos0 + lax.broadcasted_iota(I32, (ts, LANES), 0)).astype(F32)
    ang = pos * freq_ref[...]
    cos, sin = jnp.cos(ang), jnp.sin(ang)
    lane = lax.broadcasted_iota(I32, (ts, LANES), 1)
    sin_signed = jnp.where(lane < LANES // 2, -sin, sin)

    def rope(x):
        return x * cos + pltpu.roll(x, LANES // 2, 1) * sin_signed

    qr = (rope(qup[:, 256:384]) * mla_scale).astype(BF16)
    kr = rope(p[:, _OFF_KR:_OFF_KR + LANES]).astype(BF16)
    mq_ref[:, 0:128] = (qup[:, 0:128] * mla_scale).astype(BF16)
    mq_ref[:, 128:256] = qr
    mq_ref[:, 256:384] = (qup[:, 128:256] * mla_scale).astype(BF16)
    mq_ref[:, 384:512] = qr
    mk_ref[:, 0:128] = kvup[:, 0:128].astype(BF16)
    mk_ref[:, 128:256] = kr
    mk_ref[:, 256:384] = kvup[:, 128:256].astype(BF16)
    mk_ref[:, 384:512] = kr
    mv_ref[...] = kvup[:, 256:512].astype(BF16)


def _inproj(x, g, w_all, fb, qg, wuq, kvg, wukv, freq, *, seq, ts):
    t, d = x.shape
    full = lambda a: pl.BlockSpec(a.shape, lambda i: (0,) * a.ndim)
    row = lambda w: pl.BlockSpec((ts, w), lambda i: (i, 0))
    outs = ([jax.ShapeDtypeStruct((t, 384), BF16)] * 6 + [jax.ShapeDtypeStruct((t, LANES), F32)]
            + [jax.ShapeDtypeStruct((t, 512), BF16)] * 2 + [jax.ShapeDtypeStruct((t, 256), BF16)])
    return pl.pallas_call(
        functools.partial(_inproj_kernel, seq),
        grid=(t // ts,),
        in_specs=[row(d), full(g), full(w_all), full(fb), full(qg), full(wuq), full(kvg), full(wukv), full(freq)],
        out_specs=[row(384)] * 6 + [row(LANES)] + [row(512)] * 2 + [row(256)],
        out_shape=outs,
        compiler_params=_cparams(("arbitrary",), 52),
        name="inproj",
    )(x, g, w_all, fb, qg, wuq, kvg, wukv, freq)


def _split3(x):
    hi = x.astype(BF16)
    r = x - hi.astype(F32)
    mid = r.astype(BF16)
    lo = (r - mid.astype(F32)).astype(BF16)
    return hi, mid, lo


def _cumsum_kernel(logf_ref, c_ref):
    seq = logf_ref.shape[1]
    r = lax.broadcasted_iota(I32, (LANES, LANES), 0)
    c = lax.broadcasted_iota(I32, (LANES, LANES), 1)
    tri = jnp.where(r <= c, 1.0, 0.0).astype(BF16)
    carry = jnp.zeros((LANES, 1), F32)
    for i in range(seq // LANES):
        chunk = logf_ref[0, i * LANES:(i + 1) * LANES, :]
        hi, mid, lo = _split3(chunk)
        cs = _dot_tn(hi, tri) + _dot_tn(mid, tri) + _dot_tn(lo, tri) + carry
        c_ref[0, :, i * LANES:(i + 1) * LANES] = cs[0:8, :]
        carry = cs[:, LANES - 1:LANES]


def _fox_cumsum(logf):
    b, s, _ = logf.shape
    return pl.pallas_call(
        _cumsum_kernel,
        grid=(b,),
        in_specs=[pl.BlockSpec((1, s, LANES), lambda i: (i, 0, 0))],
        out_specs=pl.BlockSpec((1, 8, s), lambda i: (i, 0, 0)),
        out_shape=jax.ShapeDtypeStruct((b, 8, s), F32),
        compiler_params=_cparams(("arbitrary",), 16),
        name="fox_cumsum",
    )(logf)


def _head_masks(kind, wq, pair):
    lane = lax.broadcasted_iota(I32, (1, wq), 1)
    masks = []
    for e in range(2):
        m = (lane >= HEAD_DIM * e) & (lane < HEAD_DIM * (e + 1))
        if kind == "mla":
            head = 2 * pair + e
            rl = jnp.bitwise_and(lane - LANES, LANES // 2 - 1)
            m = m | ((lane >= LANES) & (rl >= 16 * head) & (rl < 16 * (head + 1)))
        masks.append(m)
    return masks


def _softmax_attn_kernel(kind, bk, *refs):
    if kind == "fox":
        q_ref, k_ref, v_ref, c_ref, o_ref = refs
    else:
        q_ref, k_ref, v_ref, o_ref = refs
    tq, wq = q_ref.shape[1], q_ref.shape[2]
    pair, qi = pl.program_id(1), pl.program_id(2)
    q2 = q_ref[0]
    zero = jnp.zeros_like(q2)
    qs = [jnp.where(m, q2, zero) for m in _head_masks(kind, wq, pair)]
    row = qi * tq + lax.broadcasted_iota(I32, (tq, 1), 0)
    nkb = lax.div((qi + 1) * tq + bk - 1, bk)

    def body(kb, carry):
        k0 = pl.multiple_of(kb * bk, bk)
        k2 = k_ref[0, pl.ds(k0, bk), :]
        v2 = v_ref[0, pl.ds(k0, bk), :]
        valid = (k0 + lax.broadcasted_iota(I32, (1, bk), 1)) <= row
        out = []
        for e in range(2):
            m, l, acc = carry[3 * e:3 * e + 3]
            s = _dot_nt(qs[e], k2)
            if kind == "fox":
                s = s - c_ref[0, pl.ds(2 * pair + e, 1), pl.ds(k0, bk)]
            s = jnp.where(valid, s, -jnp.inf)
            m_new = jnp.maximum(m, jnp.max(s, axis=1, keepdims=True))
            alpha = jnp.exp(m - m_new)
            p = jnp.exp(s - m_new)
            l = alpha * l + jnp.sum(p, axis=1, keepdims=True)
            acc = alpha * acc + _dot(p.astype(BF16), v2)
            out += [m_new, l, acc]
        return tuple(out)

    init = (jnp.full((tq, 1), -jnp.inf, F32), jnp.zeros((tq, 1), F32), jnp.zeros((tq, LANES), F32)) * 2
    res = lax.fori_loop(0, nkb, body, init)
    lane = lax.broadcasted_iota(I32, (tq, LANES), 1)
    o_ref[0] = jnp.where(lane < HEAD_DIM, res[2] / res[1], res[5] / res[4])


def _sb_attn_kernel(bk, q_ref, k_ref, v_ref, o_ref):
    tq, wq = q_ref.shape[1], q_ref.shape[2]
    pair, qi = pl.program_id(1), pl.program_id(2)
    q2 = q_ref[0]
    zero = jnp.zeros_like(q2)
    qs = [jnp.where(m, q2, zero) for m in _head_masks("sb", wq, pair)]
    row = qi * tq + lax.broadcasted_iota(I32, (tq, 1), 0)
    nkb = lax.div((qi + 1) * tq + bk - 1, bk)
    jj = lax.broadcasted_iota(I32, (bk, bk), 0)
    ss = lax.broadcasted_iota(I32, (bk, bk), 1)
    later_mat = jnp.where(jj > ss, 1.0, 0.0).astype(BF16)

    def body(it, carry):
        kb = nkb - 1 - it
        k0 = pl.multiple_of(kb * bk, bk)
        k2 = k_ref[0, pl.ds(k0, bk), :]
        v2 = v_ref[0, pl.ds(k0, bk), :]
        valid = (k0 + lax.broadcasted_iota(I32, (1, bk), 1)) < row
        out = []
        for e in range(2):
            run, acc = carry[2 * e:2 * e + 2]
            z = _dot_nt(qs[e], k2)
            sp = jnp.maximum(z, 0.0) + jnp.log1p(jnp.exp(-jnp.abs(z)))
            log_keep = jnp.where(valid, -sp, 0.0)
            hi = log_keep.astype(BF16)
            lo = (log_keep - hi.astype(F32)).astype(BF16)
            later = _dot(hi, later_mat) + _dot(lo, later_mat) + run
            w = jnp.where(valid, jnp.exp(z - sp + later), 0.0)
            acc = acc + _dot(w.astype(BF16), v2)
            run = run + jnp.sum(log_keep, axis=1, keepdims=True)
            out += [run, acc]
        return tuple(out)

    init = (jnp.zeros((tq, 1), F32), jnp.zeros((tq, LANES), F32)) * 2
    res = lax.fori_loop(0, nkb, body, init)
    lane = lax.broadcasted_iota(I32, (tq, LANES), 1)
    o_ref[0] = jnp.where(lane < HEAD_DIM, res[1], res[3])


def _pair_attention(kind, q, k, v, c=None, *, tq, bk):
    b, s, wtot = q.shape
    npairs = v.shape[2] // LANES
    wq = wtot // npairs
    in_specs = [pl.BlockSpec((1, tq, wq), lambda bb, p, i: (bb, i, p)),
                pl.BlockSpec((1, s, wq), lambda bb, p, i: (bb, 0, p)),
                pl.BlockSpec((1, s, LANES), lambda bb, p, i: (bb, 0, p))]
    args = [q, k, v]
    if kind == "fox":
        in_specs.append(pl.BlockSpec((1, 8, s), lambda bb, p, i: (bb, 0, 0)))
        args.append(c)
    body = functools.partial(_sb_attn_kernel, bk) if kind == "sb" else functools.partial(_softmax_attn_kernel, kind, bk)
    return pl.pallas_call(
        body,
        grid=(b, npairs, s // tq),
        in_specs=in_specs,
        out_specs=pl.BlockSpec((1, tq, LANES), lambda bb, p, i: (bb, i, p)),
        out_shape=jax.ShapeDtypeStruct((b, s, npairs * LANES), F32),
        compiler_params=_cparams(("arbitrary",) * 3, 40),
        name=kind + "_attention",
    )(*args)


def _outproj_kernel(x_ref, yf_ref, ym_ref, ys_ref, gf_ref, gm_ref, gs_ref, wf_ref, wm_ref, ws_ref, o_ref):
    y = _dot(_rms(yf_ref[...], gf_ref[...]).astype(BF16), wf_ref[...])
    y += _dot(_rms(ym_ref[...], gm_ref[...]).astype(BF16), wm_ref[...])
    y += _dot(_rms(ys_ref[...], gs_ref[...]).astype(BF16), ws_ref[...])
    o_ref[...] = x_ref[...] + y


def _outproj(x, yf, ym, ys, gf, gm, gs, wf, wm, ws, *, ts):
    t, d = x.shape
    full = lambda a: pl.BlockSpec(a.shape, lambda i: (0,) * a.ndim)
    row = lambda a: pl.BlockSpec((ts, a.shape[1]), lambda i: (i, 0))
    return pl.pallas_call(
        _outproj_kernel,
        grid=(t // ts,),
        in_specs=[row(x), row(yf), row(ym), row(ys), full(gf), full(gm), full(gs), full(wf), full(wm), full(ws)],
        out_specs=row(x),
        out_shape=jax.ShapeDtypeStruct((t, d), F32),
        compiler_params=_cparams(("arbitrary",), 40),
        name="outproj",
    )(x, yf, ym, ys, gf, gm, gs, wf, wm, ws)


def _extract_topk(items):
    def body(i, c):
        for work_ref, row_ids, val_ref, idx_ref in items:
            s = work_ref[...]
            m = jnp.max(s, axis=0, keepdims=True)
            idx = jnp.min(jnp.where(s == m, row_ids, _ID_SENTINEL), axis=0, keepdims=True)
            work_ref[...] = jnp.where(row_ids == idx, -jnp.inf, s)
            val_ref[pl.ds(i, 1), :] = m
            idx_ref[pl.ds(i, 1), :] = idx
        return c

    lax.fori_loop(0, PEER_TOPK, body, 0)


_ID_SENTINEL = PEER_TOPK * PEER_TOPK
_CAND_GROUPS = [(0, 0), (0, 8)] + [(j, 0) for j in range(1, 8)]
_N_CAND_ROWS = 8 * (len(_CAND_GROUPS) + 1)


def _candidate_cells(tr):
    r = lax.broadcasted_iota(I32, (_N_CAND_ROWS, tr), 0)
    grp, i = lax.shift_right_logical(r, 3), jnp.bitwise_and(r, 7)
    last = len(_CAND_GROUPS)
    j = jnp.where(grp <= 1, 0, jnp.where(grp < last, grp - 1, 8 + i))
    k = jnp.where(grp == 1, 8 + i, jnp.where(grp == last, 0, i))
    return j * PEER_TOPK + k


def _route_kernel(x_ref, g_ref, wq_ref, k1_ref, k2_ref, h_ref, ids_ref, gates_ref,
                  q_s, work1, work2, top_s, idx_s, best_v, best_p, ids_t, gates_t):
    k = PEER_TOPK
    tr = x_ref.shape[0]
    hb = _rms(x_ref[...], g_ref[...]).astype(BF16)
    h_ref[...] = hb
    q = _dot(hb, wq_ref[...]).astype(BF16)
    for i in range(2 * PEER_HEADS):
        q_s[i] = q[:, i * PEER_HALF:(i + 1) * PEER_HALF]
    key_rows = lax.broadcasted_iota(I32, (PEER_N_KEYS, tr), 0)
    cand_id = _candidate_cells(tr)
    row8 = lax.broadcasted_iota(I32, (8, tr), 0)

    def head(h, c):
        for half, key_ref in enumerate((k1_ref, k2_ref)):
            work1[half] = _dot_nt(key_ref[...], q_s[2 * h + half])
        _extract_topk([(work1.at[half], key_rows, top_s.at[half], idx_s.at[half]) for half in range(2)])
        for g, (j, k0) in enumerate(_CAND_GROUPS):
            sums = top_s[0, j:j + 1, :] + top_s[1, k0:k0 + 8, :]
            n_ok = PEER_TOPK // (j + 1) - k0
            work2[8 * g:8 * g + 8, :] = sums if n_ok >= 8 else jnp.where(row8 < n_ok, sums, -jnp.inf)
        work2[_N_CAND_ROWS - 8:, :] = top_s[0, 8:16, :] + top_s[1, 0:1, :]
        _extract_topk([(work2, cand_id, best_v, best_p)])
        pos = best_p[...]
        pj, pk = lax.shift_right_logical(pos, 4), jnp.bitwise_and(pos, k - 1)
        a = jnp.zeros_like(pos)
        b = jnp.zeros_like(pos)
        for j in range(k):
            a = jnp.where(pj == j, idx_s[0, j:j + 1, :], a)
            b = jnp.where(pk == j, idx_s[1, j:j + 1, :], b)
        best = best_v[...]
        e = jnp.exp(best - best[0:1, :])
        r0 = pl.multiple_of(h * k, k)
        ids_t[pl.ds(r0, k), :] = (a * PEER_N_KEYS + b).astype(F32)
        gates_t[pl.ds(r0, k), :] = e / jnp.sum(e, axis=0, keepdims=True)
        return c

    lax.fori_loop(0, PEER_HEADS, head, 0)
    ids_ref[...] = ids_t[...].T.astype(I32)
    gates_ref[...] = gates_t[...].T


def _peer_route(x, g, wq, k1, k2, *, tr):
    t, d = x.shape
    full = lambda a: pl.BlockSpec(a.shape, lambda i: (0,) * a.ndim)
    row = lambda w: pl.BlockSpec((tr, w), lambda i: (i, 0))
    k = PEER_TOPK
    return pl.pallas_call(
        _route_kernel,
        grid=(t // tr,),
        in_specs=[row(d), full(g), full(wq), full(k1), full(k2)],
        out_specs=[row(d), row(LANES), row(LANES)],
        out_shape=[jax.ShapeDtypeStruct((t, d), BF16), jax.ShapeDtypeStruct((t, LANES), I32),
                   jax.ShapeDtypeStruct((t, LANES), F32)],
        scratch_shapes=[pltpu.VMEM((2 * PEER_HEADS, tr, PEER_HALF), BF16),
                        pltpu.VMEM((2, PEER_N_KEYS, tr), F32), pltpu.VMEM((_N_CAND_ROWS, tr), F32),
                        pltpu.VMEM((2, k, tr), F32), pltpu.VMEM((2, k, tr), I32),
                        pltpu.VMEM((k, tr), F32), pltpu.VMEM((k, tr), I32),
                        pltpu.VMEM((PEER_HEADS * k, tr), F32), pltpu.VMEM((PEER_HEADS * k, tr), F32)],
        compiler_params=_cparams(("arbitrary",), 40),
        name="peer_route",
    )(x, g, wq, k1, k2)


def _gelu(x):
    return 0.5 * x * (1.0 + lax.erf(x * np.float32(np.sqrt(0.5))))


def _peer_kernel(final_norm, n_blocks, h_ref, ids_ref, gates_ref, u_ref, v_ref, x_ref, fg_ref, o_ref,
                 gate_s, a_s, acc_s):
    tq = h_ref.shape[0]
    nb = u_ref.shape[0]
    n_chunks = nb // PEER_CHUNK
    j = pl.program_id(1)

    def activate(block, dst):
        for c in range(n_chunks):
            cols = slice(c * PEER_CHUNK, (c + 1) * PEER_CHUNK)
            act = _dot_nt(h_ref[...], u_ref[cols, :])
            word = gate_s[pl.ds(block * n_chunks + c, tq, stride=GATE_PITCH), :]
            g_even = pltpu.bitcast(lax.shift_left(word, jnp.uint32(16)), F32)
            g_odd = pltpu.bitcast(jnp.bitwise_and(word, jnp.uint32(0xFFFF0000)), F32)
            a = jnp.concatenate([_gelu(act[:, :LANES]) * g_even, _gelu(act[:, LANES:]) * g_odd], axis=1)
            dst[:, cols] = a.astype(BF16)

    def mix(src):
        acc_s[...] += _dot(src[...], v_ref[...])

    @pl.when(j == 0)
    def _():
        key = lax.broadcasted_iota(I32, (PEER_N_KEYS, LANES), 0)

        def build(grp, c):
            t0 = pl.multiple_of(grp * GATE_UNROLL, GATE_UNROLL)
            ids8 = ids_ref[pl.ds(t0, GATE_UNROLL), :]
            gate8 = gates_ref[pl.ds(t0, GATE_UNROLL), :]
            i1 = lax.shift_right_logical(ids8, 7)
            i2 = jnp.bitwise_and(ids8, PEER_N_KEYS - 1)
            for k in range(GATE_UNROLL):
                p = jnp.where(i1[k:k + 1] == key, gate8[k:k + 1], 0.0).astype(BF16)
                q = jnp.where(i2[k:k + 1] == key, 1.0, 0.0).astype(BF16)
                g = _dot_nt(p, q).astype(BF16)
                row = pl.multiple_of((t0 + k) * GATE_PITCH, 8)
                gate_s[pl.ds(row, PEER_N_KEYS // 2), :] = pltpu.bitcast(g, jnp.uint32)
            return c

        lax.fori_loop(0, tq // GATE_UNROLL, build, 0)
        acc_s[...] = jnp.zeros_like(acc_s)
        activate(0, a_s.at[0])

    for parity in range(2):
        @pl.when((j >= 1) & (j < n_blocks) & (lax.rem(j, 2) == parity))
        def _():
            mix(a_s.at[1 - parity])
            activate(j, a_s.at[parity])

    @pl.when(j == n_blocks)
    def _():
        mix(a_s.at[(n_blocks - 1) % 2])
        y = x_ref[...] + acc_s[...]
        o_ref[...] = _rms(y, fg_ref[...]) if final_norm else y


def _peer_main(h, ids, gates, u, v, x, fg, *, final_norm, tq, nb):
    t, d = x.shape
    n_blocks = u.shape[0] // nb
    tok = lambda w: pl.BlockSpec((tq, w), lambda i, j: (i, 0))
    return pl.pallas_call(
        functools.partial(_peer_kernel, final_norm, n_blocks),
        grid=(t // tq, n_blocks + 1),
        in_specs=[tok(d), tok(LANES), tok(LANES),
                  pl.BlockSpec((nb, d), lambda i, j: (jnp.minimum(j, n_blocks - 1), 0)),
                  pl.BlockSpec((nb, d), lambda i, j: (jnp.maximum(j - 1, 0), 0)),
                  tok(d), pl.BlockSpec((1, d), lambda i, j: (0, 0))],
        out_specs=tok(d),
        out_shape=jax.ShapeDtypeStruct((t, d), F32),
        scratch_shapes=[pltpu.VMEM((tq * GATE_PITCH, LANES), jnp.uint32), pltpu.VMEM((2, tq, nb), BF16),
                        pltpu.VMEM((tq, d), F32)],
        compiler_params=_cparams(("arbitrary", "arbitrary"), 58),
        name="peer_main",
    )(h, ids, gates, u, v, x, fg)


def _prep_w_in(w_in, fgate_b):
    d = w_in.shape[0]
    o = np.cumsum([0, 384, 384, 384, FOX_HEADS, MLA_Q_RANK, MLA_KV_RANK, MLA_ROPE_DIM, 384, 384, 384])
    fq, fk, fv, fl, cq, ckv, kr, sq, sk, sv = [w_in[:, o[i]:o[i + 1]] for i in range(10)]
    half = MLA_ROPE_DIM // 2
    kr4 = jnp.concatenate([kr[:, :half]] * MLA_HEADS + [kr[:, half:]] * MLA_HEADS, axis=1)
    fl_pad = jnp.pad(fl, ((0, 0), (0, LANES - FOX_HEADS)))
    w_all = jnp.concatenate([fq, fk, fv, sq, sk, sv, cq, ckv, fl_pad, kr4], axis=1).astype(BF16)
    assert w_all.shape == (d, _IN_COLS_PADDED)
    fb = jnp.pad(fgate_b, (0, LANES - FOX_HEADS)).reshape(1, LANES)
    return w_all, fb


def _prep_w_uq(w_uq):
    per = MLA_NOPE_DIM + MLA_ROPE_DIM
    half = MLA_ROPE_DIM // 2
    nope = [w_uq[:, h * per:h * per + MLA_NOPE_DIM] for h in range(MLA_HEADS)]
    x1 = [w_uq[:, h * per + MLA_NOPE_DIM:h * per + MLA_NOPE_DIM + half] for h in range(MLA_HEADS)]
    x2 = [w_uq[:, h * per + MLA_NOPE_DIM + half:(h + 1) * per] for h in range(MLA_HEADS)]
    return jnp.concatenate(nope + x1 + x2, axis=1).astype(BF16)


def _prep_w_ukv(w_ukv):
    per = MLA_NOPE_DIM + MLA_V_DIM
    kn = [w_ukv[:, h * per:h * per + MLA_NOPE_DIM] for h in range(MLA_HEADS)]
    vv = [w_ukv[:, h * per + MLA_NOPE_DIM:(h + 1) * per] for h in range(MLA_HEADS)]
    return jnp.concatenate(kn + vv, axis=1).astype(BF16)


def _rope_lane_freq():
    half = MLA_ROPE_DIM // 2
    inv_freq = 1.0 / (ROPE_THETA ** (jnp.arange(half, dtype=F32) / half))
    return jnp.tile(inv_freq, LANES // half).reshape(1, LANES)


def kernel(x, attn_norm_g, w_in, fgate_b, mla_q_norm_g, w_uq, mla_kv_norm_g, w_ukv, out_norm_fox, out_norm_mla, out_norm_sb, w_out, ffn_norm_g, peer_w_query, peer_keys1, peer_keys2, peer_u, peer_v, final_norm_g):
    b, s, d = x.shape
    depth = w_in.shape[0]
    t = b * s
    xt = x.reshape(t, d)
    freq = _rope_lane_freq()
    row = lambda a: a.reshape(1, -1)
    for l in range(depth):
        w_all, fb = _prep_w_in(w_in[l], fgate_b[l])
        fq, fk, fv, sq, sk, sv, logf, mq, mk, mv = _inproj(
            xt, row(attn_norm_g[l]), w_all, fb, row(mla_q_norm_g[l]), _prep_w_uq(w_uq[l]),
            row(mla_kv_norm_g[l]), _prep_w_ukv(w_ukv[l]), freq, seq=s, ts=512)
        bs = lambda a: a.reshape(b, s, a.shape[1])
        c = _fox_cumsum(bs(logf))
        y_fox = _pair_attention("fox", bs(fq), bs(fk), bs(fv), c, tq=256, bk=512)
        y_mla = _pair_attention("mla", bs(mq), bs(mk), bs(mv), tq=256, bk=512)
        y_sb = _pair_attention("sb", bs(sq), bs(sk), bs(sv), tq=256, bk=256)
        wo = w_out[l].astype(BF16)
        xt = _outproj(xt, y_fox.reshape(t, -1), y_mla.reshape(t, -1), y_sb.reshape(t, -1),
                      row(out_norm_fox[l]), row(out_norm_mla[l]), row(out_norm_sb[l]),
                      wo[0:384], wo[384:640], wo[640:1024], ts=512)
        h2, ids, gates = _peer_route(xt, row(ffn_norm_g[l]), peer_w_query[l].astype(BF16),
                                     peer_keys1[l].astype(BF16), peer_keys2[l].astype(BF16), tr=512)
        xt = _peer_main(h2, ids, gates, peer_u[l].astype(BF16), peer_v[l].astype(BF16), xt,
                        row(final_norm_g), final_norm=(l == depth - 1), tq=512, nb=2048)
    return xt.reshape(b, s, d)
```

```python
import functools

import jax
import jax.numpy as jnp
import numpy as np
from jax import lax
from jax.experimental import pallas as pl
from jax.experimental.pallas import tpu as pltpu

F32, BF16, I32 = jnp.float32, jnp.bfloat16, jnp.int32

NORM_EPS = 1e-6
HEAD_DIM = 64
FOX_HEADS = 6
SB_HEADS = 6
MLA_HEADS = 4
MLA_Q_RANK = 256
MLA_KV_RANK = 128
MLA_NOPE_DIM = 64
MLA_ROPE_DIM = 32
MLA_V_DIM = 64
ROPE_THETA = 10000.0
PEER_HEADS = 8
PEER_N_KEYS = 128
PEER_TOPK = 16
PEER_HALF = 128

LOG2E = float(np.log2(np.e))
LANES = 128
MIB = 1024 * 1024
GATE_PITCH = 72
GATE_UNROLL = 16
PEER_CHUNK = 256


def _cparams(sem, vmem_mib):
    return pltpu.CompilerParams(dimension_semantics=sem, vmem_limit_bytes=vmem_mib * MIB)


def _rms(x, g):
    return x * lax.rsqrt(jnp.mean(x * x, axis=-1, keepdims=True) + NORM_EPS) * g


def _log_sigmoid(y):
    return jnp.minimum(y, 0.0) - jnp.log1p(jnp.exp(-jnp.abs(y)))


def _dot(a, b):
    return jnp.dot(a, b, preferred_element_type=F32)


def _dot_nt(a, b):
    return lax.dot_general(a, b, (((1,), (1,)), ((), ())), preferred_element_type=F32)


def _dot_tn(a, b):
    return lax.dot_general(a, b, (((0,), (0,)), ((), ())), preferred_element_type=F32)


_W384 = FOX_HEADS * HEAD_DIM
_OFF_FQ, _OFF_FK, _OFF_FV = 0, 384, 768
_OFF_SQ, _OFF_SK, _OFF_SV = 1152, 1536, 1920
_OFF_CQ, _OFF_CKV, _OFF_FLOG, _OFF_KR = 2304, 2560, 2688, 2816
_IN_COLS_PADDED = 2944


def _inproj_kernel(seq, x_ref, g_ref, w_ref, fb_ref, qg_ref, wuq_ref, kvg_ref, wukv_ref, freq_ref,
                   fq_ref, fk_ref, fv_ref, sq_ref, sk_ref, sv_ref, logf_ref, mq_ref, mk_ref, mv_ref):
    ts = x_ref.shape[0]
    h = _rms(x_ref[...], g_ref[...]).astype(BF16)
    p = _dot(h, w_ref[...])
    qk_scale = HEAD_DIM ** -0.5 * LOG2E
    fq_ref[...] = (p[:, _OFF_FQ:_OFF_FQ + 384] * qk_scale).astype(BF16)
    fk_ref[...] = p[:, _OFF_FK:_OFF_FK + 384].astype(BF16)
    fv_ref[...] = p[:, _OFF_FV:_OFF_FV + 384].astype(BF16)
    sq_ref[...] = (p[:, _OFF_SQ:_OFF_SQ + 384] * qk_scale).astype(BF16)
    sk_ref[...] = p[:, _OFF_SK:_OFF_SK + 384].astype(BF16)
    sv_ref[...] = p[:, _OFF_SV:_OFF_SV + 384].astype(BF16)
    logf_ref[...] = _log_sigmoid(p[:, _OFF_FLOG:_OFF_FLOG + LANES] + fb_ref[...])

    mla_scale = (MLA_NOPE_DIM + MLA_ROPE_DIM) ** -0.5 * LOG2E
    cq = _rms(p[:, _OFF_CQ:_OFF_CQ + MLA_Q_RANK], qg_ref[...]).astype(BF16)
    qup = _dot(cq, wuq_ref[...])
    ckv = _rms(p[:, _OFF_CKV:_OFF_CKV + MLA_KV_RANK], kvg_ref[...]).astype(BF16)
    kvup = _dot(ckv, wukv_ref[...])

    pos0 = lax.rem(pl.program_id(0) * ts, seq)
    pos = (pos0 + lax.broadcasted_iota(I32, (ts, LANES), 0)).astype(F32)
    ang = pos * freq_ref[...]
    cos, sin = jnp.cos(ang), jnp.sin(ang)
    lane = lax.broadcasted_iota(I32, (ts, LANES), 1)
    sin_signed = jnp.where(lane < LANES // 2, -sin, sin)

    def rope(x):
        return x * cos + pltpu.roll(x, LANES // 2, 1) * sin_signed

    qr = (rope(qup[:, 256:384]) * mla_scale).astype(BF16)
    kr = rope(p[:, _OFF_KR:_OFF_KR + LANES]).astype(BF16)
    mq_ref[:, 0:128] = (qup[:, 0:128] * mla_scale).astype(BF16)
    mq_ref[:, 128:256] = qr
    mq_ref[:, 256:384] = (qup[:, 128:256] * mla_scale).astype(BF16)
    mq_ref[:, 384:512] = qr
    mk_ref[:, 0:128] = kvup[:, 0:128].astype(BF16)
    mk_ref[:, 128:256] = kr
    mk_ref[:, 256:384] = kvup[:, 128:256].astype(BF16)
    mk_ref[:, 384:512] = kr
    mv_ref[...] = kvup[:, 256:512].astype(BF16)


def _inproj(x, g, w_all, fb, qg, wuq, kvg, wukv, freq, *, seq, ts):
    t, d = x.shape
    full = lambda a: pl.BlockSpec(a.shape, lambda i: (0,) * a.ndim)
    row = lambda w: pl.BlockSpec((ts, w), lambda i: (i, 0))
    outs = ([jax.ShapeDtypeStruct((t, 384), BF16)] * 6 + [jax.ShapeDtypeStruct((t, LANES), F32)]
            + [jax.ShapeDtypeStruct((t, 512), BF16)] * 2 + [jax.ShapeDtypeStruct((t, 256), BF16)])
    return pl.pallas_call(
        functools.partial(_inproj_kernel, seq),
        grid=(t // ts,),
        in_specs=[row(d), full(g), full(w_all), full(fb), full(qg), full(wuq), full(kvg), full(wukv), full(freq)],
        out_specs=[row(384)] * 6 + [row(LANES)] + [row(512)] * 2 + [row(256)],
        out_shape=outs,
        compiler_params=_cparams(("arbitrary",), 52),
        name="inproj",
    )(x, g, w_all, fb, qg, wuq, kvg, wukv, freq)


def _split3(x):
    hi = x.astype(BF16)
    r = x - hi.astype(F32)
    mid = r.astype(BF16)
    lo = (r - mid.astype(F32)).astype(BF16)
    return hi, mid, lo


def _cumsum_kernel(logf_ref, c_ref):
    seq = logf_ref.shape[1]
    r = lax.broadcasted_iota(I32, (LANES, LANES), 0)
    c = lax.broadcasted_iota(I32, (LANES, LANES), 1)
    tri = jnp.where(r <= c, 1.0, 0.0).astype(BF16)
    carry = jnp.zeros((LANES, 1), F32)
    for i in range(seq // LANES):
        chunk = logf_ref[0, i * LANES:(i + 1) * LANES, :]
        hi, mid, lo = _split3(chunk)
        cs = _dot_tn(hi, tri) + _dot_tn(mid, tri) + _dot_tn(lo, tri) + carry
        c_ref[0, :, i * LANES:(i + 1) * LANES] = cs[0:8, :] * LOG2E
        carry = cs[:, LANES - 1:LANES]


def _fox_cumsum(logf):
    b, s, _ = logf.shape
    return pl.pallas_call(
        _cumsum_kernel,
        grid=(b,),
        in_specs=[pl.BlockSpec((1, s, LANES), lambda i: (i, 0, 0))],
        out_specs=pl.BlockSpec((1, 8, s), lambda i: (i, 0, 0)),
        out_shape=jax.ShapeDtypeStruct((b, 8, s), F32),
        compiler_params=_cparams(("arbitrary",), 16),
        name="fox_cumsum",
    )(logf)


def _head_masks(kind, wq, pair):
    lane = lax.broadcasted_iota(I32, (1, wq), 1)
    masks = []
    for e in range(2):
        m = (lane >= HEAD_DIM * e) & (lane < HEAD_DIM * (e + 1))
        if kind == "mla":
            head = 2 * pair + e
            rl = jnp.bitwise_and(lane - LANES, LANES // 2 - 1)
            m = m | ((lane >= LANES) & (rl >= 16 * head) & (rl < 16 * (head + 1)))
        masks.append(m)
    return masks


def _softmax_attn_kernel(kind, bk, *refs):
    if kind == "fox":
        q_ref, k_ref, v_ref, c_ref, o_ref = refs
    else:
        q_ref, k_ref, v_ref, o_ref = refs
    tq, wq = q_ref.shape[1], q_ref.shape[2]
    pair, qi = pl.program_id(1), pl.program_id(2)
    q2 = q_ref[0]
    zero = jnp.zeros_like(q2)
    qs = [jnp.where(m, q2, zero) for m in _head_masks(kind, wq, pair)]
    row = qi * tq + lax.broadcasted_iota(I32, (tq, 1), 0)
    n_full = lax.div(qi * tq, bk)
    n_diag = max(1, tq // bk)

    def block(kb, carry, masked):
        k0 = pl.multiple_of(kb * bk, bk)
        k2 = k_ref[0, pl.ds(k0, bk), :]
        v2 = v_ref[0, pl.ds(k0, bk), :]
        out = []
        for e in range(2):
            m, l, acc = carry[3 * e:3 * e + 3]
            s = _dot_nt(qs[e], k2)
            if kind == "fox":
                s = s - c_ref[0, pl.ds(2 * pair + e, 1), pl.ds(k0, bk)]
            if masked:
                valid = (k0 + lax.broadcasted_iota(I32, (1, bk), 1)) <= row
                s = jnp.where(valid, s, -jnp.inf)
            m_new = jnp.maximum(m, jnp.max(s, axis=1, keepdims=True))
            alpha = jnp.exp2(m - m_new)
            p = jnp.exp2(s - m_new)
            l = alpha * l + jnp.sum(p, axis=1, keepdims=True)
            acc = alpha * acc + _dot(p.astype(BF16), v2)
            out += [m_new, l, acc]
        return tuple(out)

    carry = (jnp.full((tq, 1), -jnp.inf, F32), jnp.zeros((tq, 1), F32), jnp.zeros((tq, LANES), F32)) * 2
    carry = lax.fori_loop(0, n_full, lambda kb, c: block(kb, c, False), carry)
    for d in range(n_diag):
        carry = block(n_full + d, carry, True)
    lane = lax.broadcasted_iota(I32, (tq, LANES), 1)
    o_ref[0] = jnp.where(lane < HEAD_DIM, carry[2] / carry[1], carry[5] / carry[4])


def _sb_attn_kernel(bk, q_ref, k_ref, v_ref, o_ref):
    tq, wq = q_ref.shape[1], q_ref.shape[2]
    pair, qi = pl.program_id(1), pl.program_id(2)
    q2 = q_ref[0]
    zero = jnp.zeros_like(q2)
    qs = [jnp.where(m, q2, zero) for m in _head_masks("sb", wq, pair)]
    row = qi * tq + lax.broadcasted_iota(I32, (tq, 1), 0)
    n_full = lax.div(qi * tq, bk)
    n_diag = max(1, tq // bk)
    jj = lax.broadcasted_iota(I32, (bk, bk), 0)
    ss = lax.broadcasted_iota(I32, (bk, bk), 1)
    later_mat = jnp.where(jj > ss, 1.0, 0.0).astype(BF16)

    def block(kb, carry, masked):
        k0 = pl.multiple_of(kb * bk, bk)
        k2 = k_ref[0, pl.ds(k0, bk), :]
        v2 = v_ref[0, pl.ds(k0, bk), :]
        out = []
        for e in range(2):
            run, acc = carry[2 * e:2 * e + 2]
            z = _dot_nt(qs[e], k2)
            drop = jnp.maximum(z, 0.0) + jnp.log2(1.0 + jnp.exp2(-jnp.abs(z)))
            logw = z - drop
            if masked:
                valid = (k0 + lax.broadcasted_iota(I32, (1, bk), 1)) < row
                drop = jnp.where(valid, drop, 0.0)
            hi = drop.astype(BF16)
            lo = (drop - hi.astype(F32)).astype(BF16)
            later = _dot(hi, later_mat) + _dot(lo, later_mat) + run
            w = jnp.exp2(logw - later)
            if masked:
                w = jnp.where(valid, w, 0.0)
            acc = acc + _dot(w.astype(BF16), v2)
            run = run + jnp.sum(drop, axis=1, keepdims=True)
            out += [run, acc]
        return tuple(out)

    carry = (jnp.zeros((tq, 1), F32), jnp.zeros((tq, LANES), F32)) * 2
    for d in reversed(range(n_diag)):
        carry = block(n_full + d, carry, True)
    assert n_diag % 2 == 0

    def two_blocks(it, c):
        kb = n_full - 1 - 2 * it
        return block(kb - 1, block(kb, c, False), False)

    carry = lax.fori_loop(0, lax.div(n_full, 2), two_blocks, carry)
    lane = lax.broadcasted_iota(I32, (tq, LANES), 1)
    o_ref[0] = jnp.where(lane < HEAD_DIM, carry[1], carry[3])


def _pair_attention(kind, q, k, v, c=None, *, tq, bk):
    b, s, wtot = q.shape
    npairs = v.shape[2] // LANES
    wq = wtot // npairs
    in_specs = [pl.BlockSpec((1, tq, wq), lambda bb, p, i: (bb, i, p)),
                pl.BlockSpec((1, s, wq), lambda bb, p, i: (bb, 0, p)),
                pl.BlockSpec((1, s, LANES), lambda bb, p, i: (bb, 0, p))]
    args = [q, k, v]
    if kind == "fox":
        in_specs.append(pl.BlockSpec((1, 8, s), lambda bb, p, i: (bb, 0, 0)))
        args.append(c)
    body = functools.partial(_sb_attn_kernel, bk) if kind == "sb" else functools.partial(_softmax_attn_kernel, kind, bk)
    return pl.pallas_call(
        body,
        grid=(b, npairs, s // tq),
        in_specs=in_specs,
        out_specs=pl.BlockSpec((1, tq, LANES), lambda bb, p, i: (bb, i, p)),
        out_shape=jax.ShapeDtypeStruct((b, s, npairs * LANES), F32),
        compiler_params=_cparams(("arbitrary",) * 3, 40),
        name=kind + "_attention",
    )(*args)


def _outproj_kernel(x_ref, yf_ref, ym_ref, ys_ref, gf_ref, gm_ref, gs_ref, wf_ref, wm_ref, ws_ref, o_ref):
    y = _dot(_rms(yf_ref[...], gf_ref[...]).astype(BF16), wf_ref[...])
    y += _dot(_rms(ym_ref[...], gm_ref[...]).astype(BF16), wm_ref[...])
    y += _dot(_rms(ys_ref[...], gs_ref[...]).astype(BF16), ws_ref[...])
    o_ref[...] = x_ref[...] + y


def _outproj(x, yf, ym, ys, gf, gm, gs, wf, wm, ws, *, ts):
    t, d = x.shape
    full = lambda a: pl.BlockSpec(a.shape, lambda i: (0,) * a.ndim)
    row = lambda a: pl.BlockSpec((ts, a.shape[1]), lambda i: (i, 0))
    return pl.pallas_call(
        _outproj_kernel,
        grid=(t // ts,),
        in_specs=[row(x), row(yf), row(ym), row(ys), full(gf), full(gm), full(gs), full(wf), full(wm), full(ws)],
        out_specs=row(x),
        out_shape=jax.ShapeDtypeStruct((t, d), F32),
        compiler_params=_cparams(("arbitrary",), 40),
        name="outproj",
    )(x, yf, ym, ys, gf, gm, gs, wf, wm, ws)


def _extract_topk(items):
    def body(i, c):
        for work_ref, row_ids, val_ref, idx_ref in items:
            s = work_ref[...]
            m = jnp.max(s, axis=0, keepdims=True)
            idx = jnp.min(jnp.where(s == m, row_ids, _ID_SENTINEL), axis=0, keepdims=True)
            work_ref[...] = jnp.where(row_ids == idx, -jnp.inf, s)
            val_ref[pl.ds(i, 1), :] = m
            idx_ref[pl.ds(i, 1), :] = idx
        return c

    lax.fori_loop(0, PEER_TOPK, body, 0)


_ID_SENTINEL = float(PEER_TOPK * PEER_TOPK)
_CAND_GROUPS = [(0, 0), (0, 8)] + [(j, 0) for j in range(1, 8)]
_N_CAND_ROWS = 8 * (len(_CAND_GROUPS) + 1)


def _candidate_cells(tr):
    r = lax.broadcasted_iota(I32, (_N_CAND_ROWS, tr), 0)
    grp, i = lax.shift_right_logical(r, 3), jnp.bitwise_and(r, 7)
    last = len(_CAND_GROUPS)
    j = jnp.where(grp <= 1, 0, jnp.where(grp < last, grp - 1, 8 + i))
    k = jnp.where(grp == 1, 8 + i, jnp.where(grp == last, 0, i))
    return (j * PEER_TOPK + k).astype(F32)


def _route_kernel(x_ref, g_ref, wq_ref, k1_ref, k2_ref, h_ref, ids_ref, gates_ref,
                  q_s, work1, work2, top_s, idx_s, best_v, best_p, ids_t, gates_t):
    k = PEER_TOPK
    tr = x_ref.shape[0]
    hb = _rms(x_ref[...], g_ref[...]).astype(BF16)
    h_ref[...] = hb
    q = _dot(hb, wq_ref[...]).astype(BF16)
    for i in range(2 * PEER_HEADS):
        q_s[i] = q[:, i * PEER_HALF:(i + 1) * PEER_HALF]
    key_rows = lax.broadcasted_iota(I32, (PEER_N_KEYS, tr), 0).astype(F32)
    cand_id = _candidate_cells(tr)
    row8 = lax.broadcasted_iota(I32, (8, tr), 0)

    def head(h, c):
        for half, key_ref in enumerate((k1_ref, k2_ref)):
            work1[half] = _dot_nt(key_ref[...], q_s[2 * h + half])
        _extract_topk([(work1.at[half], key_rows, top_s.at[half], idx_s.at[half]) for half in range(2)])
        for g, (j, k0) in enumerate(_CAND_GROUPS):
            sums = top_s[0, j:j + 1, :] + top_s[1, k0:k0 + 8, :]
            n_ok = PEER_TOPK // (j + 1) - k0
            work2[8 * g:8 * g + 8, :] = sums if n_ok >= 8 else jnp.where(row8 < n_ok, sums, -jnp.inf)
        work2[_N_CAND_ROWS - 8:, :] = top_s[0, 8:16, :] + top_s[1, 0:1, :]
        _extract_topk([(work2, cand_id, best_v, best_p)])
        pos = best_p[...].astype(I32)
        pj, pk = lax.shift_right_logical(pos, 4), jnp.bitwise_and(pos, k - 1)
        a = jnp.zeros(pos.shape, F32)
        b = jnp.zeros(pos.shape, F32)
        for j in range(k):
            a = jnp.where(pj == j, idx_s[0, j:j + 1, :], a)
            b = jnp.where(pk == j, idx_s[1, j:j + 1, :], b)
        best = best_v[...]
        e = jnp.exp(best - best[0:1, :])
        r0 = pl.multiple_of(h * k, k)
        ids_t[pl.ds(r0, k), :] = a * PEER_N_KEYS + b
        gates_t[pl.ds(r0, k), :] = e / jnp.sum(e, axis=0, keepdims=True)
        return c

    lax.fori_loop(0, PEER_HEADS, head, 0)
    ids_ref[...] = ids_t[...].T.astype(I32)
    gates_ref[...] = gates_t[...].T


def _peer_route(x, g, wq, k1, k2, *, tr):
    t, d = x.shape
    full = lambda a: pl.BlockSpec(a.shape, lambda i: (0,) * a.ndim)
    row = lambda w: pl.BlockSpec((tr, w), lambda i: (i, 0))
    k = PEER_TOPK
    return pl.pallas_call(
        _route_kernel,
        grid=(t // tr,),
        in_specs=[row(d), full(g), full(wq), full(k1), full(k2)],
        out_specs=[row(d), row(LANES), row(LANES)],
        out_shape=[jax.ShapeDtypeStruct((t, d), BF16), jax.ShapeDtypeStruct((t, LANES), I32),
                   jax.ShapeDtypeStruct((t, LANES), F32)],
        scratch_shapes=[pltpu.VMEM((2 * PEER_HEADS, tr, PEER_HALF), BF16),
                        pltpu.VMEM((2, PEER_N_KEYS, tr), F32), pltpu.VMEM((_N_CAND_ROWS, tr), F32),
                        pltpu.VMEM((2, k, tr), F32), pltpu.VMEM((2, k, tr), F32),
                        pltpu.VMEM((k, tr), F32), pltpu.VMEM((k, tr), F32),
                        pltpu.VMEM((PEER_HEADS * k, tr), F32), pltpu.VMEM((PEER_HEADS * k, tr), F32)],
        compiler_params=_cparams(("arbitrary",), 40),
        name="peer_route",
    )(x, g, wq, k1, k2)


def _gelu(x):
    return 0.5 * x * (1.0 + lax.erf(x * np.float32(np.sqrt(0.5))))


def _peer_kernel(final_norm, n_blocks, h_ref, ids_ref, gates_ref, u_ref, v_ref, x_ref, fg_ref, o_ref,
                 gate_s, a_s, acc_s):
    tq = h_ref.shape[0]
    nb = u_ref.shape[0]
    n_chunks = nb // PEER_CHUNK
    j = pl.program_id(1)

    def activate(block, dst):
        for c in range(n_chunks):
            cols = slice(c * PEER_CHUNK, (c + 1) * PEER_CHUNK)
            act = _dot_nt(h_ref[...], u_ref[cols, :])
            word = gate_s[pl.ds(block * n_chunks + c, tq, stride=GATE_PITCH), :]
            g_even = pltpu.bitcast(lax.shift_left(word, jnp.uint32(16)), F32)
            g_odd = pltpu.bitcast(jnp.bitwise_and(word, jnp.uint32(0xFFFF0000)), F32)
            a = jnp.concatenate([_gelu(act[:, :LANES]) * g_even, _gelu(act[:, LANES:]) * g_odd], axis=1)
            dst[:, cols] = a.astype(BF16)

    def mix(src):
        acc_s[...] += _dot(src[...], v_ref[...])

    @pl.when(j == 0)
    def _():
        key = lax.broadcasted_iota(I32, (PEER_N_KEYS, LANES), 0)

        def build(grp, c):
            t0 = pl.multiple_of(grp * GATE_UNROLL, GATE_UNROLL)
            ids8 = ids_ref[pl.ds(t0, GATE_UNROLL), :]
            gate8 = gates_ref[pl.ds(t0, GATE_UNROLL), :]
            i1 = lax.shift_right_logical(ids8, 7)
            i2 = jnp.bitwise_and(ids8, PEER_N_KEYS - 1)
            for k in range(GATE_UNROLL):
                p = jnp.where(i1[k:k + 1] == key, gate8[k:k + 1], 0.0).astype(BF16)
                q = jnp.where(i2[k:k + 1] == key, 1.0, 0.0).astype(BF16)
                g = _dot_nt(p, q).astype(BF16)
                row = pl.multiple_of((t0 + k) * GATE_PITCH, 8)
                gate_s[pl.ds(row, PEER_N_KEYS // 2), :] = pltpu.bitcast(g, jnp.uint32)
            return c

        lax.fori_loop(0, tq // GATE_UNROLL, build, 0)
        acc_s[...] = jnp.zeros_like(acc_s)
        activate(0, a_s.at[0])

    for parity in range(2):
        @pl.when((j >= 1) & (j < n_blocks) & (lax.rem(j, 2) == parity))
        def _():
            mix(a_s.at[1 - parity])
            activate(j, a_s.at[parity])

    @pl.when(j == n_blocks)
    def _():
        mix(a_s.at[(n_blocks - 1) % 2])
        y = x_ref[...] + acc_s[...]
        o_ref[...] = _rms(y, fg_ref[...]) if final_norm else y


def _peer_main(h, ids, gates, u, v, x, fg, *, final_norm, tq, nb):
    t, d = x.shape
    n_blocks = u.shape[0] // nb
    tok = lambda w: pl.BlockSpec((tq, w), lambda i, j: (i, 0))
    return pl.pallas_call(
        functools.partial(_peer_kernel, final_norm, n_blocks),
        grid=(t // tq, n_blocks + 1),
        in_specs=[tok(d), tok(LANES), tok(LANES),
                  pl.BlockSpec((nb, d), lambda i, j: (jnp.minimum(j, n_blocks - 1), 0)),
                  pl.BlockSpec((nb, d), lambda i, j: (jnp.maximum(j - 1, 0), 0)),
                  tok(d), pl.BlockSpec((1, d), lambda i, j: (0, 0))],
        out_specs=tok(d),
        out_shape=jax.ShapeDtypeStruct((t, d), F32),
        scratch_shapes=[pltpu.VMEM((tq * GATE_PITCH, LANES), jnp.uint32), pltpu.VMEM((2, tq, nb), BF16),
                        pltpu.VMEM((tq, d), F32)],
        compiler_params=_cparams(("arbitrary", "arbitrary"), 58),
        name="peer_main",
    )(h, ids, gates, u, v, x, fg)


def _prep_w_in(w_in, fgate_b):
    d = w_in.shape[0]
    o = np.cumsum([0, 384, 384, 384, FOX_HEADS, MLA_Q_RANK, MLA_KV_RANK, MLA_ROPE_DIM, 384, 384, 384])
    fq, fk, fv, fl, cq, ckv, kr, sq, sk, sv = [w_in[:, o[i]:o[i + 1]] for i in range(10)]
    half = MLA_ROPE_DIM // 2
    kr4 = jnp.concatenate([kr[:, :half]] * MLA_HEADS + [kr[:, half:]] * MLA_HEADS, axis=1)
    fl_pad = jnp.pad(fl, ((0, 0), (0, LANES - FOX_HEADS)))
    w_all = jnp.concatenate([fq, fk, fv, sq, sk, sv, cq, ckv, fl_pad, kr4], axis=1).astype(BF16)
    assert w_all.shape == (d, _IN_COLS_PADDED)
    fb = jnp.pad(fgate_b, (0, LANES - FOX_HEADS)).reshape(1, LANES)
    return w_all, fb


def _prep_w_uq(w_uq):
    per = MLA_NOPE_DIM + MLA_ROPE_DIM
    half = MLA_ROPE_DIM // 2
    nope = [w_uq[:, h * per:h * per + MLA_NOPE_DIM] for h in range(MLA_HEADS)]
    x1 = [w_uq[:, h * per + MLA_NOPE_DIM:h * per + MLA_NOPE_DIM + half] for h in range(MLA_HEADS)]
    x2 = [w_uq[:, h * per + MLA_NOPE_DIM + half:(h + 1) * per] for h in range(MLA_HEADS)]
    return jnp.concatenate(nope + x1 + x2, axis=1).astype(BF16)


def _prep_w_ukv(w_ukv):
    per = MLA_NOPE_DIM + MLA_V_DIM
    kn = [w_ukv[:, h * per:h * per + MLA_NOPE_DIM] for h in range(MLA_HEADS)]
    vv = [w_ukv[:, h * per + MLA_NOPE_DIM:(h + 1) * per] for h in range(MLA_HEADS)]
    return jnp.concatenate(kn + vv, axis=1).astype(BF16)


def _rope_lane_freq():
    half = MLA_ROPE_DIM // 2
    inv_freq = 1.0 / (ROPE_THETA ** (jnp.arange(half, dtype=F32) / half))
    return jnp.tile(inv_freq, LANES // half).reshape(1, LANES)


def kernel(x, attn_norm_g, w_in, fgate_b, mla_q_norm_g, w_uq, mla_kv_norm_g, w_ukv, out_norm_fox, out_norm_mla, out_norm_sb, w_out, ffn_norm_g, peer_w_query, peer_keys1, peer_keys2, peer_u, peer_v, final_norm_g):
    b, s, d = x.shape
    depth = w_in.shape[0]
    t = b * s
    xt = x.reshape(t, d)
    freq = _rope_lane_freq()
    row = lambda a: a.reshape(1, -1)
    for l in range(depth):
        w_all, fb = _prep_w_in(w_in[l], fgate_b[l])
        fq, fk, fv, sq, sk, sv, logf, mq, mk, mv = _inproj(
            xt, row(attn_norm_g[l]), w_all, fb, row(mla_q_norm_g[l]), _prep_w_uq(w_uq[l]),
            row(mla_kv_norm_g[l]), _prep_w_ukv(w_ukv[l]), freq, seq=s, ts=512)
        bs = lambda a: a.reshape(b, s, a.shape[1])
        c = _fox_cumsum(bs(logf))
        y_fox = _pair_attention("fox", bs(fq), bs(fk), bs(fv), c, tq=512, bk=512)
        y_mla = _pair_attention("mla", bs(mq), bs(mk), bs(mv), tq=512, bk=512)
        y_sb = _pair_attention("sb", bs(sq), bs(sk), bs(sv), tq=512, bk=256)
        wo = w_out[l].astype(BF16)
        xt = _outproj(xt, y_fox.reshape(t, -1), y_mla.reshape(t, -1), y_sb.reshape(t, -1),
                      row(out_norm_fox[l]), row(out_norm_mla[l]), row(out_norm_sb[l]),
                      wo[0:384], wo[384:640], wo[640:1024], ts=512)
        h2, ids, gates = _peer_route(xt, row(ffn_norm_g[l]), peer_w_query[l].astype(BF16),
                                     peer_keys1[l].astype(BF16), peer_keys2[l].astype(BF16), tr=512)
        xt = _peer_main(h2, ids, gates, peer_u[l].astype(BF16), peer_v[l].astype(BF16), xt,
                        row(final_norm_g), final_norm=(l == depth - 1), tq=512, nb=2048)
    return xt.reshape(b, s, d)
```

```python
import functools

import jax
import jax.numpy as jnp
import numpy as np
from jax import lax
from jax.experimental import pallas as pl
from jax.experimental.pallas import tpu as pltpu

F32, BF16, I32 = jnp.float32, jnp.bfloat16, jnp.int32

NORM_EPS = 1e-6
HEAD_DIM = 64
FOX_HEADS = 6
SB_HEADS = 6
MLA_HEADS = 4
MLA_Q_RANK = 256
MLA_KV_RANK = 128
MLA_NOPE_DIM = 64
MLA_ROPE_DIM = 32
MLA_V_DIM = 64
ROPE_THETA = 10000.0
PEER_HEADS = 8
PEER_N_KEYS = 128
PEER_TOPK = 16
PEER_HALF = 128

LOG2E = float(np.log2(np.e))
LANES = 128
MIB = 1024 * 1024
GATE_PITCH = 72
GATE_UNROLL = 16
PEER_CHUNK = 256


def _cparams(sem, vmem_mib):
    return pltpu.CompilerParams(dimension_semantics=sem, vmem_limit_bytes=vmem_mib * MIB)


def _rms(x, g):
    return x * lax.rsqrt(jnp.mean(x * x, axis=-1, keepdims=True) + NORM_EPS) * g


def _log_sigmoid(y):
    return jnp.minimum(y, 0.0) - jnp.log1p(jnp.exp(-jnp.abs(y)))


def _dot(a, b):
    return jnp.dot(a, b, preferred_element_type=F32)


def _dot_nt(a, b):
    return lax.dot_general(a, b, (((1,), (1,)), ((), ())), preferred_element_type=F32)


def _dot_tn(a, b):
    return lax.dot_general(a, b, (((0,), (0,)), ((), ())), preferred_element_type=F32)


_W384 = FOX_HEADS * HEAD_DIM
_OFF_FQ, _OFF_FK, _OFF_FV = 0, 384, 768
_OFF_SQ, _OFF_SK, _OFF_SV = 1152, 1536, 1920
_OFF_CQ, _OFF_CKV, _OFF_FLOG, _OFF_KR = 2304, 2560, 2688, 2816
_IN_COLS_PADDED = 2944


def _inproj_kernel(seq, x_ref, g_ref, w_ref, fb_ref, qg_ref, wuq_ref, kvg_ref, wukv_ref, freq_ref,
                   fq_ref, fk_ref, fv_ref, sq_ref, sk_ref, sv_ref, logf_ref, mq_ref, mk_ref, mv_ref):
    ts = x_ref.shape[0]
    h = _rms(x_ref[...], g_ref[...]).astype(BF16)
    p = _dot(h, w_ref[...])
    qk_scale = HEAD_DIM ** -0.5 * LOG2E
    fq_ref[...] = (p[:, _OFF_FQ:_OFF_FQ + 384] * qk_scale).astype(BF16)
    fk_ref[...] = p[:, _OFF_FK:_OFF_FK + 384].astype(BF16)
    fv_ref[...] = p[:, _OFF_FV:_OFF_FV + 384].astype(BF16)
    sq_ref[...] = (p[:, _OFF_SQ:_OFF_SQ + 384] * qk_scale).astype(BF16)
    sk_ref[...] = p[:, _OFF_SK:_OFF_SK + 384].astype(BF16)
    sv_ref[...] = p[:, _OFF_SV:_OFF_SV + 384].astype(BF16)
    logf_ref[...] = _log_sigmoid(p[:, _OFF_FLOG:_OFF_FLOG + LANES] + fb_ref[...])

    mla_scale = (MLA_NOPE_DIM + MLA_ROPE_DIM) ** -0.5 * LOG2E
    cq = _rms(p[:, _OFF_CQ:_OFF_CQ + MLA_Q_RANK], qg_ref[...]).astype(BF16)
    qup = _dot(cq, wuq_ref[...])
    ckv = _rms(p[:, _OFF_CKV:_OFF_CKV + MLA_KV_RANK], kvg_ref[...]).astype(BF16)
    kvup = _dot(ckv, wukv_ref[...])

    pos0 = lax.rem(pl.program_id(0) * ts, seq)
    pos = (pos0 + lax.broadcasted_iota(I32, (ts, LANES), 0)).astype(F32)
    ang = pos * freq_ref[...]
    cos, sin = jnp.cos(ang), jnp.sin(ang)
    lane = lax.broadcasted_iota(I32, (ts, LANES), 1)
    sin_signed = jnp.where(lane < LANES // 2, -sin, sin)

    def rope(x):
        return x * cos + pltpu.roll(x, LANES // 2, 1) * sin_signed

    qr = (rope(qup[:, 256:384]) * mla_scale).astype(BF16)
    kr = rope(p[:, _OFF_KR:_OFF_KR + LANES]).astype(BF16)
    mq_ref[:, 0:128] = (qup[:, 0:128] * mla_scale).astype(BF16)
    mq_ref[:, 128:256] = qr
    mq_ref[:, 256:384] = (qup[:, 128:256] * mla_scale).astype(BF16)
    mq_ref[:, 384:512] = qr
    mk_ref[:, 0:128] = kvup[:, 0:128].astype(BF16)
    mk_ref[:, 128:256] = kr
    mk_ref[:, 256:384] = kvup[:, 128:256].astype(BF16)
    mk_ref[:, 384:512] = kr
    mv_ref[...] = kvup[:, 256:512].astype(BF16)


def _inproj(x, g, w_all, fb, qg, wuq, kvg, wukv, freq, *, seq, ts):
    t, d = x.shape
    full = lambda a: pl.BlockSpec(a.shape, lambda i: (0,) * a.ndim)
    row = lambda w: pl.BlockSpec((ts, w), lambda i: (i, 0))
    outs = ([jax.ShapeDtypeStruct((t, 384), BF16)] * 6 + [jax.ShapeDtypeStruct((t, LANES), F32)]
            + [jax.ShapeDtypeStruct((t, 512), BF16)] * 2 + [jax.ShapeDtypeStruct((t, 256), BF16)])
    return pl.pallas_call(
        functools.partial(_inproj_kernel, seq),
        grid=(t // ts,),
        in_specs=[row(d), full(g), full(w_all), full(fb), full(qg), full(wuq), full(kvg), full(wukv), full(freq)],
        out_specs=[row(384)] * 6 + [row(LANES)] + [row(512)] * 2 + [row(256)],
        out_shape=outs,
        compiler_params=_cparams(("arbitrary",), 52),
        name="inproj",
    )(x, g, w_all, fb, qg, wuq, kvg, wukv, freq)


def _split3(x):
    hi = x.astype(BF16)
    r = x - hi.astype(F32)
    mid = r.astype(BF16)
    lo = (r - mid.astype(F32)).astype(BF16)
    return hi, mid, lo


def _cumsum_kernel(logf_ref, c_ref):
    seq = logf_ref.shape[1]
    r = lax.broadcasted_iota(I32, (LANES, LANES), 0)
    c = lax.broadcasted_iota(I32, (LANES, LANES), 1)
    tri = jnp.where(r <= c, 1.0, 0.0).astype(BF16)
    carry = jnp.zeros((LANES, 1), F32)
    for i in range(seq // LANES):
        chunk = logf_ref[0, i * LANES:(i + 1) * LANES, :]
        hi, mid, lo = _split3(chunk)
        cs = _dot_tn(hi, tri) + _dot_tn(mid, tri) + _dot_tn(lo, tri) + carry
        c_ref[0, :, i * LANES:(i + 1) * LANES] = cs[0:8, :] * LOG2E
        carry = cs[:, LANES - 1:LANES]


def _fox_cumsum(logf):
    b, s, _ = logf.shape
    return pl.pallas_call(
        _cumsum_kernel,
        grid=(b,),
        in_specs=[pl.BlockSpec((1, s, LANES), lambda i: (i, 0, 0))],
        out_specs=pl.BlockSpec((1, 8, s), lambda i: (i, 0, 0)),
        out_shape=jax.ShapeDtypeStruct((b, 8, s), F32),
        compiler_params=_cparams(("arbitrary",), 16),
        name="fox_cumsum",
    )(logf)


def _head_masks(kind, wq, pair):
    lane = lax.broadcasted_iota(I32, (1, wq), 1)
    masks = []
    for e in range(2):
        m = (lane >= HEAD_DIM * e) & (lane < HEAD_DIM * (e + 1))
        if kind == "mla":
            head = 2 * pair + e
            rl = jnp.bitwise_and(lane - LANES, LANES // 2 - 1)
            m = m | ((lane >= LANES) & (rl >= 16 * head) & (rl < 16 * (head + 1)))
        masks.append(m)
    return masks


def _softmax_attn_kernel(kind, bk, *refs):
    if kind == "fox":
        q_ref, k_ref, v_ref, c_ref, o_ref = refs
    else:
        q_ref, k_ref, v_ref, o_ref = refs
    tq, wq = q_ref.shape[1], q_ref.shape[2]
    pair, qi = pl.program_id(1), pl.program_id(2)
    q2 = q_ref[0]
    zero = jnp.zeros_like(q2)
    qs = [jnp.where(m, q2, zero) for m in _head_masks(kind, wq, pair)]
    row = qi * tq + lax.broadcasted_iota(I32, (tq, 1), 0)
    n_full = lax.div(qi * tq, bk)
    n_diag = max(1, tq // bk)

    def block(kb, carry, masked):
        k0 = pl.multiple_of(kb * bk, bk)
        k2 = k_ref[0, pl.ds(k0, bk), :]
        v2 = v_ref[0, pl.ds(k0, bk), :]
        out = []
        for e in range(2):
            m, l, acc = carry[3 * e:3 * e + 3]
            s = _dot_nt(qs[e], k2)
            if kind == "fox":
                s = s - c_ref[0, pl.ds(2 * pair + e, 1), pl.ds(k0, bk)]
            if masked:
                valid = (k0 + lax.broadcasted_iota(I32, (1, bk), 1)) <= row
                s = jnp.where(valid, s, -jnp.inf)
            m_new = jnp.maximum(m, jnp.max(s, axis=1, keepdims=True))
            alpha = jnp.exp2(m - m_new)
            p = jnp.exp2(s - m_new)
            l = alpha * l + jnp.sum(p, axis=1, keepdims=True)
            acc = alpha * acc + _dot(p.astype(BF16), v2)
            out += [m_new, l, acc]
        return tuple(out)

    carry = (jnp.full((tq, 1), -jnp.inf, F32), jnp.zeros((tq, 1), F32), jnp.zeros((tq, LANES), F32)) * 2
    carry = lax.fori_loop(0, n_full, lambda kb, c: block(kb, c, False), carry)
    for d in range(n_diag):
        carry = block(n_full + d, carry, True)
    lane = lax.broadcasted_iota(I32, (tq, LANES), 1)
    o_ref[0] = jnp.where(lane < HEAD_DIM, carry[2] / carry[1], carry[5] / carry[4])


def _sb_attn_kernel(bk, q_ref, k_ref, v_ref, o_ref):
    tq, wq = q_ref.shape[1], q_ref.shape[2]
    pair, qi = pl.program_id(1), pl.program_id(2)
    q2 = q_ref[0]
    zero = jnp.zeros_like(q2)
    qs = [jnp.where(m, q2, zero) for m in _head_masks("sb", wq, pair)]
    row = qi * tq + lax.broadcasted_iota(I32, (tq, 1), 0)
    n_full = lax.div(qi * tq, bk)
    n_diag = max(1, tq // bk)
    jj = lax.broadcasted_iota(I32, (bk, bk), 0)
    ss = lax.broadcasted_iota(I32, (bk, bk), 1)
    later_mat = jnp.where(jj > ss, 1.0, 0.0).astype(BF16)

    def block(kb, carry, masked):
        k0 = pl.multiple_of(kb * bk, bk)
        k2 = k_ref[0, pl.ds(k0, bk), :]
        v2 = v_ref[0, pl.ds(k0, bk), :]
        out = []
        for e in range(2):
            run, acc = carry[2 * e:2 * e + 2]
            z = _dot_nt(qs[e], k2)
            drop = jnp.maximum(z, 0.0) + jnp.log2(1.0 + jnp.exp2(-jnp.abs(z)))
            logw = z - drop
            if masked:
                valid = (k0 + lax.broadcasted_iota(I32, (1, bk), 1)) < row
                drop = jnp.where(valid, drop, 0.0)
            hi = drop.astype(BF16)
            lo = (drop - hi.astype(F32)).astype(BF16)
            later = _dot(hi, later_mat) + _dot(lo, later_mat) + run
            w = jnp.exp2(logw - later)
            if masked:
                w = jnp.where(valid, w, 0.0)
            acc = acc + _dot(w.astype(BF16), v2)
            run = run + jnp.sum(drop, axis=1, keepdims=True)
            out += [run, acc]
        return tuple(out)

    carry = (jnp.zeros((tq, 1), F32), jnp.zeros((tq, LANES), F32)) * 2
    for d in reversed(range(n_diag)):
        carry = block(n_full + d, carry, True)
    assert n_diag % 2 == 0

    def two_blocks(it, c):
        kb = n_full - 1 - 2 * it
        return block(kb - 1, block(kb, c, False), False)

    carry = lax.fori_loop(0, lax.div(n_full, 2), two_blocks, carry)
    lane = lax.broadcasted_iota(I32, (tq, LANES), 1)
    o_ref[0] = jnp.where(lane < HEAD_DIM, carry[1], carry[3])


def _pair_attention(kind, q, k, v, c=None, *, tq, bk):
    b, s, wtot = q.shape
    npairs = v.shape[2] // LANES
    wq = wtot // npairs
    in_specs = [pl.BlockSpec((1, tq, wq), lambda bb, p, i: (bb, i, p)),
                pl.BlockSpec((1, s, wq), lambda bb, p, i: (bb, 0, p)),
                pl.BlockSpec((1, s, LANES), lambda bb, p, i: (bb, 0, p))]
    args = [q, k, v]
    if kind == "fox":
        in_specs.append(pl.BlockSpec((1, 8, s), lambda bb, p, i: (bb, 0, 0)))
        args.append(c)
    body = functools.partial(_sb_attn_kernel, bk) if kind == "sb" else functools.partial(_softmax_attn_kernel, kind, bk)
    return pl.pallas_call(
        body,
        grid=(b, npairs, s // tq),
        in_specs=in_specs,
        out_specs=pl.BlockSpec((1, tq, LANES), lambda bb, p, i: (bb, i, p)),
        out_shape=jax.ShapeDtypeStruct((b, s, npairs * LANES), F32),
        compiler_params=_cparams(("arbitrary",) * 3, 40),
        name=kind + "_attention",
    )(*args)


def _outproj_kernel(x_ref, yf_ref, ym_ref, ys_ref, gf_ref, gm_ref, gs_ref, wf_ref, wm_ref, ws_ref, o_ref):
    y = _dot(_rms(yf_ref[...], gf_ref[...]).astype(BF16), wf_ref[...])
    y += _dot(_rms(ym_ref[...], gm_ref[...]).astype(BF16), wm_ref[...])
    y += _dot(_rms(ys_ref[...], gs_ref[...]).astype(BF16), ws_ref[...])
    o_ref[...] = x_ref[...] + y


def _outproj(x, yf, ym, ys, gf, gm, gs, wf, wm, ws, *, ts):
    t, d = x.shape
    full = lambda a: pl.BlockSpec(a.shape, lambda i: (0,) * a.ndim)
    row = lambda a: pl.BlockSpec((ts, a.shape[1]), lambda i: (i, 0))
    return pl.pallas_call(
        _outproj_kernel,
        grid=(t // ts,),
        in_specs=[row(x), row(yf), row(ym), row(ys), full(gf), full(gm), full(gs), full(wf), full(wm), full(ws)],
        out_specs=row(x),
        out_shape=jax.ShapeDtypeStruct((t, d), F32),
        compiler_params=_cparams(("arbitrary",), 40),
        name="outproj",
    )(x, yf, ym, ys, gf, gm, gs, wf, wm, ws)


def _extract_topk(items):
    def body(i, c):
        for work_ref, row_ids, val_ref, idx_ref in items:
            s = work_ref[...]
            m = jnp.max(s, axis=0, keepdims=True)
            idx = jnp.min(jnp.where(s == m, row_ids, _ID_SENTINEL), axis=0, keepdims=True)
            work_ref[...] = jnp.where(row_ids == idx, -jnp.inf, s)
            val_ref[pl.ds(i, 1), :] = m
            idx_ref[pl.ds(i, 1), :] = idx
        return c

    lax.fori_loop(0, PEER_TOPK, body, 0)


_ID_SENTINEL = float(PEER_TOPK * PEER_TOPK)
_CAND_GROUPS = [(0, 0), (0, 8)] + [(j, 0) for j in range(1, 8)]
_N_CAND_ROWS = 8 * (len(_CAND_GROUPS) + 1)


def _candidate_cells(tr):
    r = lax.broadcasted_iota(I32, (_N_CAND_ROWS, tr), 0)
    grp, i = lax.shift_right_logical(r, 3), jnp.bitwise_and(r, 7)
    last = len(_CAND_GROUPS)
    j = jnp.where(grp <= 1, 0, jnp.where(grp < last, grp - 1, 8 + i))
    k = jnp.where(grp == 1, 8 + i, jnp.where(grp == last, 0, i))
    return (j * PEER_TOPK + k).astype(F32)


def _route_kernel(x_ref, g_ref, wq_ref, k1_ref, k2_ref, h_ref, ids_ref, gates_ref,
                  q_s, work1, work2, top_s, idx_s, best_v, best_p, ids_t, gates_t):
    k = PEER_TOPK
    tr = x_ref.shape[0]
    hb = _rms(x_ref[...], g_ref[...]).astype(BF16)
    h_ref[...] = hb
    q = _dot(hb, wq_ref[...]).astype(BF16)
    for i in range(2 * PEER_HEADS):
        q_s[i] = q[:, i * PEER_HALF:(i + 1) * PEER_HALF]
    key_rows = lax.broadcasted_iota(I32, (PEER_N_KEYS, tr), 0).astype(F32)
    cand_id = _candidate_cells(tr)
    row8 = lax.broadcasted_iota(I32, (8, tr), 0)

    def head(h, c):
        for half, key_ref in enumerate((k1_ref, k2_ref)):
            work1[half] = _dot_nt(key_ref[...], q_s[2 * h + half])
        _extract_topk([(work1.at[half], key_rows, top_s.at[half], idx_s.at[half]) for half in range(2)])
        for g, (j, k0) in enumerate(_CAND_GROUPS):
            sums = top_s[0, j:j + 1, :] + top_s[1, k0:k0 + 8, :]
            n_ok = PEER_TOPK // (j + 1) - k0
            work2[8 * g:8 * g + 8, :] = sums if n_ok >= 8 else jnp.where(row8 < n_ok, sums, -jnp.inf)
        work2[_N_CAND_ROWS - 8:, :] = top_s[0, 8:16, :] + top_s[1, 0:1, :]
        _extract_topk([(work2, cand_id, best_v, best_p)])
        pos = best_p[...].astype(I32)
        pj, pk = lax.shift_right_logical(pos, 4), jnp.bitwise_and(pos, k - 1)
        a = jnp.zeros(pos.shape, F32)
        b = jnp.zeros(pos.shape, F32)
        for j in range(k):
            a = jnp.where(pj == j, idx_s[0, j:j + 1, :], a)
            b = jnp.where(pk == j, idx_s[1, j:j + 1, :], b)
        best = best_v[...]
        e = jnp.exp(best - best[0:1, :])
        r0 = pl.multiple_of(h * k, k)
        ids_t[pl.ds(r0, k), :] = a * PEER_N_KEYS + b
        gates_t[pl.ds(r0, k), :] = e / jnp.sum(e, axis=0, keepdims=True)
        return c

    lax.fori_loop(0, PEER_HEADS, head, 0)
    ids_ref[...] = ids_t[...].T.astype(I32)
    gates_ref[...] = gates_t[...].T


def _peer_route(x, g, wq, k1, k2, *, tr):
    t, d = x.shape
    full = lambda a: pl.BlockSpec(a.shape, lambda i: (0,) * a.ndim)
    row = lambda w: pl.BlockSpec((tr, w), lambda i: (i, 0))
    k = PEER_TOPK
    return pl.pallas_call(
        _route_kernel,
        grid=(t // tr,),
        in_specs=[row(d), full(g), full(wq), full(k1), full(k2)],
        out_specs=[row(d), row(LANES), row(LANES)],
        out_shape=[jax.ShapeDtypeStruct((t, d), BF16), jax.ShapeDtypeStruct((t, LANES), I32),
                   jax.ShapeDtypeStruct((t, LANES), F32)],
        scratch_shapes=[pltpu.VMEM((2 * PEER_HEADS, tr, PEER_HALF), BF16),
                        pltpu.VMEM((2, PEER_N_KEYS, tr), F32), pltpu.VMEM((_N_CAND_ROWS, tr), F32),
                        pltpu.VMEM((2, k, tr), F32), pltpu.VMEM((2, k, tr), F32),
                        pltpu.VMEM((k, tr), F32), pltpu.VMEM((k, tr), F32),
                        pltpu.VMEM((PEER_HEADS * k, tr), F32), pltpu.VMEM((PEER_HEADS * k, tr), F32)],
        compiler_params=_cparams(("arbitrary",), 40),
        name="peer_route",
    )(x, g, wq, k1, k2)


def _gelu(x):
    return 0.5 * x * (1.0 + lax.erf(x * np.float32(np.sqrt(0.5))))


def _peer_kernel(final_norm, n_blocks, h_ref, ids_ref, gates_ref, ids_next_ref, gates_next_ref, u_ref, v_ref,
                 x_ref, fg_ref, o_ref, gate_s, a_s, acc_s):
    tq = h_ref.shape[0]
    nb = u_ref.shape[0]
    d = v_ref.shape[1]
    n_chunks = nb // PEER_CHUNK
    i, j = pl.program_id(0), pl.program_id(1)
    last_tile = pl.num_programs(0) - 1
    key = lax.broadcasted_iota(I32, (PEER_N_KEYS, LANES), 0)

    def build_gates(ids_r, gates_r, t0):
        ids = ids_r[pl.ds(t0, GATE_UNROLL), :]
        gate = gates_r[pl.ds(t0, GATE_UNROLL), :]
        i1 = lax.shift_right_logical(ids, 7)
        i2 = jnp.bitwise_and(ids, PEER_N_KEYS - 1)
        for k in range(GATE_UNROLL):
            p = jnp.where(i1[k:k + 1] == key, gate[k:k + 1], 0.0).astype(BF16)
            q = jnp.where(i2[k:k + 1] == key, 1.0, 0.0).T.astype(BF16)
            g = _dot(p, q).astype(BF16)
            row = (t0 + k) * GATE_PITCH
            row = row if isinstance(row, int) else pl.multiple_of(row, 8)
            gate_s[pl.ds(row, PEER_N_KEYS // 2), :] = pltpu.bitcast(g, jnp.uint32)

    def activate(block, dst):
        for c in range(n_chunks):
            cols = slice(c * PEER_CHUNK, (c + 1) * PEER_CHUNK)
            act = _dot_nt(h_ref[...], u_ref[cols, :])
            word = gate_s[pl.ds(block * n_chunks + c, tq, stride=GATE_PITCH), :]
            g_even = pltpu.bitcast(lax.shift_left(word, jnp.uint32(16)), F32)
            g_odd = pltpu.bitcast(jnp.bitwise_and(word, jnp.uint32(0xFFFF0000)), F32)
            a = jnp.concatenate([_gelu(act[:, :LANES]) * g_even, _gelu(act[:, LANES:]) * g_odd], axis=1)
            dst[:, cols] = a.astype(BF16)

    def mix(src):
        acc_s[...] += _dot(src[...], v_ref[...])

    @pl.when((i == 0) & (j == 0))
    def _():
        def build(grp, c):
            build_gates(ids_ref, gates_ref, pl.multiple_of(grp * GATE_UNROLL, GATE_UNROLL))
            return c

        lax.fori_loop(0, tq // GATE_UNROLL, build, 0)

    @pl.when(j == 0)
    def _():
        acc_s[...] = jnp.zeros_like(acc_s)
        activate(0, a_s.at[0])

    for parity in range(2):
        @pl.when((j >= 1) & (j < n_blocks) & (lax.rem(j, 2) == parity))
        def _():
            mix(a_s.at[1 - parity])
            activate(j, a_s.at[parity])

    def finalize():
        y = x_ref[...] + acc_s[...]
        o_ref[...] = _rms(y, fg_ref[...]) if final_norm else y

    last_src = a_s.at[(n_blocks - 1) % 2]

    @pl.when((j == n_blocks) & (i < last_tile))
    def _():
        n_parts = d // PEER_CHUNK
        for part in range(n_parts):
            cols = slice(part * PEER_CHUNK, (part + 1) * PEER_CHUNK)
            acc_s[:, cols] += _dot(last_src[...], v_ref[:, cols])
            for t0 in range(part * tq // n_parts, (part + 1) * tq // n_parts, GATE_UNROLL):
                build_gates(ids_next_ref, gates_next_ref, t0)
        finalize()

    @pl.when((j == n_blocks) & (i == last_tile))
    def _():
        mix(last_src)
        finalize()


def _peer_main(h, ids, gates, u, v, x, fg, *, final_norm, tq, nb):
    t, d = x.shape
    n_blocks = u.shape[0] // nb
    n_tiles = t // tq
    tok = lambda w: pl.BlockSpec((tq, w), lambda i, j: (i, 0))
    tok_next = pl.BlockSpec((tq, LANES), lambda i, j: (jnp.minimum(i + 1, n_tiles - 1), 0))
    return pl.pallas_call(
        functools.partial(_peer_kernel, final_norm, n_blocks),
        grid=(n_tiles, n_blocks + 1),
        in_specs=[tok(d), tok(LANES), tok(LANES), tok_next, tok_next,
                  pl.BlockSpec((nb, d), lambda i, j: (jnp.minimum(j, n_blocks - 1), 0)),
                  pl.BlockSpec((nb, d), lambda i, j: (jnp.maximum(j - 1, 0), 0)),
                  tok(d), pl.BlockSpec((1, d), lambda i, j: (0, 0))],
        out_specs=tok(d),
        out_shape=jax.ShapeDtypeStruct((t, d), F32),
        scratch_shapes=[pltpu.VMEM((tq * GATE_PITCH, LANES), jnp.uint32), pltpu.VMEM((2, tq, nb), BF16),
                        pltpu.VMEM((tq, d), F32)],
        compiler_params=_cparams(("arbitrary", "arbitrary"), 58),
        name="peer_main",
    )(h, ids, gates, ids, gates, u, v, x, fg)


def _prep_w_in(w_in, fgate_b):
    d = w_in.shape[0]
    o = np.cumsum([0, 384, 384, 384, FOX_HEADS, MLA_Q_RANK, MLA_KV_RANK, MLA_ROPE_DIM, 384, 384, 384])
    fq, fk, fv, fl, cq, ckv, kr, sq, sk, sv = [w_in[:, o[i]:o[i + 1]] for i in range(10)]
    half = MLA_ROPE_DIM // 2
    kr4 = jnp.concatenate([kr[:, :half]] * MLA_HEADS + [kr[:, half:]] * MLA_HEADS, axis=1)
    fl_pad = jnp.pad(fl, ((0, 0), (0, LANES - FOX_HEADS)))
    w_all = jnp.concatenate([fq, fk, fv, sq, sk, sv, cq, ckv, fl_pad, kr4], axis=1).astype(BF16)
    assert w_all.shape == (d, _IN_COLS_PADDED)
    fb = jnp.pad(fgate_b, (0, LANES - FOX_HEADS)).reshape(1, LANES)
    return w_all, fb


def _prep_w_uq(w_uq):
    per = MLA_NOPE_DIM + MLA_ROPE_DIM
    half = MLA_ROPE_DIM // 2
    nope = [w_uq[:, h * per:h * per + MLA_NOPE_DIM] for h in range(MLA_HEADS)]
    x1 = [w_uq[:, h * per + MLA_NOPE_DIM:h * per + MLA_NOPE_DIM + half] for h in range(MLA_HEADS)]
    x2 = [w_uq[:, h * per + MLA_NOPE_DIM + half:(h + 1) * per] for h in range(MLA_HEADS)]
    return jnp.concatenate(nope + x1 + x2, axis=1).astype(BF16)


def _prep_w_ukv(w_ukv):
    per = MLA_NOPE_DIM + MLA_V_DIM
    kn = [w_ukv[:, h * per:h * per + MLA_NOPE_DIM] for h in range(MLA_HEADS)]
    vv = [w_ukv[:, h * per + MLA_NOPE_DIM:(h + 1) * per] for h in range(MLA_HEADS)]
    return jnp.concatenate(kn + vv, axis=1).astype(BF16)


def _rope_lane_freq():
    half = MLA_ROPE_DIM // 2
    inv_freq = 1.0 / (ROPE_THETA ** (jnp.arange(half, dtype=F32) / half))
    return jnp.tile(inv_freq, LANES // half).reshape(1, LANES)


def kernel(x, attn_norm_g, w_in, fgate_b, mla_q_norm_g, w_uq, mla_kv_norm_g, w_ukv, out_norm_fox, out_norm_mla, out_norm_sb, w_out, ffn_norm_g, peer_w_query, peer_keys1, peer_keys2, peer_u, peer_v, final_norm_g):
    b, s, d = x.shape
    depth = w_in.shape[0]
    t = b * s
    xt = x.reshape(t, d)
    freq = _rope_lane_freq()
    row = lambda a: a.reshape(1, -1)
    for l in range(depth):
        w_all, fb = _prep_w_in(w_in[l], fgate_b[l])
        fq, fk, fv, sq, sk, sv, logf, mq, mk, mv = _inproj(
            xt, row(attn_norm_g[l]), w_all, fb, row(mla_q_norm_g[l]), _prep_w_uq(w_uq[l]),
            row(mla_kv_norm_g[l]), _prep_w_ukv(w_ukv[l]), freq, seq=s, ts=512)
        bs = lambda a: a.reshape(b, s, a.shape[1])
        c = _fox_cumsum(bs(logf))
        y_fox = _pair_attention("fox", bs(fq), bs(fk), bs(fv), c, tq=512, bk=512)
        y_mla = _pair_attention("mla", bs(mq), bs(mk), bs(mv), tq=512, bk=512)
        y_sb = _pair_attention("sb", bs(sq), bs(sk), bs(sv), tq=512, bk=256)
        wo = w_out[l].astype(BF16)
        xt = _outproj(xt, y_fox.reshape(t, -1), y_mla.reshape(t, -1), y_sb.reshape(t, -1),
                      row(out_norm_fox[l]), row(out_norm_mla[l]), row(out_norm_sb[l]),
                      wo[0:384], wo[384:640], wo[640:1024], ts=512)
        h2, ids, gates = _peer_route(xt, row(ffn_norm_g[l]), peer_w_query[l].astype(BF16),
                                     peer_keys1[l].astype(BF16), peer_keys2[l].astype(BF16), tr=512)
        xt = _peer_main(h2, ids, gates, peer_u[l].astype(BF16), peer_v[l].astype(BF16), xt,
                        row(final_norm_g), final_norm=(l == depth - 1), tq=512, nb=2048)
    return xt.reshape(b, s, d)
```

```python
import functools

import jax
import jax.numpy as jnp
import numpy as np
from jax import lax
from jax.experimental import pallas as pl
from jax.experimental.pallas import tpu as pltpu

F32, BF16, I32 = jnp.float32, jnp.bfloat16, jnp.int32

NORM_EPS = 1e-6
HEAD_DIM = 64
FOX_HEADS = 6
SB_HEADS = 6
MLA_HEADS = 4
MLA_Q_RANK = 256
MLA_KV_RANK = 128
MLA_NOPE_DIM = 64
MLA_ROPE_DIM = 32
MLA_V_DIM = 64
ROPE_THETA = 10000.0
PEER_HEADS = 8
PEER_N_KEYS = 128
PEER_TOPK = 16
PEER_HALF = 128

LOG2E = float(np.log2(np.e))
LANES = 128
MIB = 1024 * 1024
GATE_PITCH = 72
GATE_UNROLL = 16
PEER_CHUNK = 256


def _cparams(sem, vmem_mib):
    return pltpu.CompilerParams(dimension_semantics=sem, vmem_limit_bytes=vmem_mib * MIB)


def _rms(x, g):
    return x * lax.rsqrt(jnp.mean(x * x, axis=-1, keepdims=True) + NORM_EPS) * g


def _log_sigmoid(y):
    return jnp.minimum(y, 0.0) - jnp.log1p(jnp.exp(-jnp.abs(y)))


def _dot(a, b):
    return jnp.dot(a, b, preferred_element_type=F32)


def _dot_nt(a, b):
    return lax.dot_general(a, b, (((1,), (1,)), ((), ())), preferred_element_type=F32)


def _dot_tn(a, b):
    return lax.dot_general(a, b, (((0,), (0,)), ((), ())), preferred_element_type=F32)


_W384 = FOX_HEADS * HEAD_DIM
_OFF_FQ, _OFF_FK, _OFF_FV = 0, 384, 768
_OFF_SQ, _OFF_SK, _OFF_SV = 1152, 1536, 1920
_OFF_CQ, _OFF_CKV, _OFF_FLOG, _OFF_KR = 2304, 2560, 2688, 2816
_IN_COLS_PADDED = 2944


def _inproj_kernel(seq, x_ref, g_ref, w_ref, fb_ref, qg_ref, wuq_ref, kvg_ref, wukv_ref, freq_ref,
                   fq_ref, fk_ref, fv_ref, sq_ref, sk_ref, sv_ref, logf_ref, mq_ref, mk_ref, mv_ref):
    ts = x_ref.shape[0]
    h = _rms(x_ref[...], g_ref[...]).astype(BF16)
    p = _dot(h, w_ref[...])
    qk_scale = HEAD_DIM ** -0.5 * LOG2E
    fq_ref[...] = (p[:, _OFF_FQ:_OFF_FQ + 384] * qk_scale).astype(BF16)
    fk_ref[...] = p[:, _OFF_FK:_OFF_FK + 384].astype(BF16)
    fv_ref[...] = p[:, _OFF_FV:_OFF_FV + 384].astype(BF16)
    sq_ref[...] = (p[:, _OFF_SQ:_OFF_SQ + 384] * qk_scale).astype(BF16)
    sk_ref[...] = p[:, _OFF_SK:_OFF_SK + 384].astype(BF16)
    sv_ref[...] = p[:, _OFF_SV:_OFF_SV + 384].astype(BF16)
    logf_ref[...] = _log_sigmoid(p[:, _OFF_FLOG:_OFF_FLOG + LANES] + fb_ref[...])

    mla_scale = (MLA_NOPE_DIM + MLA_ROPE_DIM) ** -0.5 * LOG2E
    cq = _rms(p[:, _OFF_CQ:_OFF_CQ + MLA_Q_RANK], qg_ref[...]).astype(BF16)
    qup = _dot(cq, wuq_ref[...])
    ckv = _rms(p[:, _OFF_CKV:_OFF_CKV + MLA_KV_RANK], kvg_ref[...]).astype(BF16)
    kvup = _dot(ckv, wukv_ref[...])

    pos0 = lax.rem(pl.program_id(0) * ts, seq)
    pos = (pos0 + lax.broadcasted_iota(I32, (ts, LANES), 0)).astype(F32)
    ang = pos * freq_ref[...]
    cos, sin = jnp.cos(ang), jnp.sin(ang)
    lane = lax.broadcasted_iota(I32, (ts, LANES), 1)
    sin_signed = jnp.where(lane < LANES // 2, -sin, sin)

    def rope(x):
        return x * cos + pltpu.roll(x, LANES // 2, 1) * sin_signed

    qr = (rope(qup[:, 256:384]) * mla_scale).astype(BF16)
    kr = rope(p[:, _OFF_KR:_OFF_KR + LANES]).astype(BF16)
    mq_ref[:, 0:128] = (qup[:, 0:128] * mla_scale).astype(BF16)
    mq_ref[:, 128:256] = qr
    mq_ref[:, 256:384] = (qup[:, 128:256] * mla_scale).astype(BF16)
    mq_ref[:, 384:512] = qr
    mk_ref[:, 0:128] = kvup[:, 0:128].astype(BF16)
    mk_ref[:, 128:256] = kr
    mk_ref[:, 256:384] = kvup[:, 128:256].astype(BF16)
    mk_ref[:, 384:512] = kr
    mv_ref[...] = kvup[:, 256:512].astype(BF16)


def _inproj(x, g, w_all, fb, qg, wuq, kvg, wukv, freq, *, seq, ts):
    t, d = x.shape
    full = lambda a: pl.BlockSpec(a.shape, lambda i: (0,) * a.ndim)
    row = lambda w: pl.BlockSpec((ts, w), lambda i: (i, 0))
    outs = ([jax.ShapeDtypeStruct((t, 384), BF16)] * 6 + [jax.ShapeDtypeStruct((t, LANES), F32)]
            + [jax.ShapeDtypeStruct((t, 512), BF16)] * 2 + [jax.ShapeDtypeStruct((t, 256), BF16)])
    return pl.pallas_call(
        functools.partial(_inproj_kernel, seq),
        grid=(t // ts,),
        in_specs=[row(d), full(g), full(w_all), full(fb), full(qg), full(wuq), full(kvg), full(wukv), full(freq)],
        out_specs=[row(384)] * 6 + [row(LANES)] + [row(512)] * 2 + [row(256)],
        out_shape=outs,
        compiler_params=_cparams(("arbitrary",), 52),
        name="inproj",
    )(x, g, w_all, fb, qg, wuq, kvg, wukv, freq)


def _split3(x):
    hi = x.astype(BF16)
    r = x - hi.astype(F32)
    mid = r.astype(BF16)
    lo = (r - mid.astype(F32)).astype(BF16)
    return hi, mid, lo


def _cumsum_kernel(logf_ref, c_ref):
    seq = logf_ref.shape[1]
    r = lax.broadcasted_iota(I32, (LANES, LANES), 0)
    c = lax.broadcasted_iota(I32, (LANES, LANES), 1)
    tri = jnp.where(r <= c, 1.0, 0.0).astype(BF16)
    carry = jnp.zeros((LANES, 1), F32)
    for i in range(seq // LANES):
        chunk = logf_ref[0, i * LANES:(i + 1) * LANES, :]
        hi, mid, lo = _split3(chunk)
        cs = _dot_tn(hi, tri) + _dot_tn(mid, tri) + _dot_tn(lo, tri) + carry
        c_ref[0, :, i * LANES:(i + 1) * LANES] = cs[0:8, :] * LOG2E
        carry = cs[:, LANES - 1:LANES]


def _fox_cumsum(logf):
    b, s, _ = logf.shape
    return pl.pallas_call(
        _cumsum_kernel,
        grid=(b,),
        in_specs=[pl.BlockSpec((1, s, LANES), lambda i: (i, 0, 0))],
        out_specs=pl.BlockSpec((1, 8, s), lambda i: (i, 0, 0)),
        out_shape=jax.ShapeDtypeStruct((b, 8, s), F32),
        compiler_params=_cparams(("arbitrary",), 16),
        name="fox_cumsum",
    )(logf)


def _head_masks(kind, wq, pair):
    lane = lax.broadcasted_iota(I32, (1, wq), 1)
    masks = []
    for e in range(2):
        m = (lane >= HEAD_DIM * e) & (lane < HEAD_DIM * (e + 1))
        if kind == "mla":
            head = 2 * pair + e
            rl = jnp.bitwise_and(lane - LANES, LANES // 2 - 1)
            m = m | ((lane >= LANES) & (rl >= 16 * head) & (rl < 16 * (head + 1)))
        masks.append(m)
    return masks


def _softmax_attn_kernel(kind, bk, *refs):
    if kind == "fox":
        q_ref, k_ref, v_ref, c_ref, o_ref = refs
    else:
        q_ref, k_ref, v_ref, o_ref = refs
    tq, wq = q_ref.shape[1], q_ref.shape[2]
    pair, qi = pl.program_id(1), pl.program_id(2)
    q2 = q_ref[0]
    zero = jnp.zeros_like(q2)
    qs = [jnp.where(m, q2, zero) for m in _head_masks(kind, wq, pair)]
    row = qi * tq + lax.broadcasted_iota(I32, (tq, 1), 0)
    n_full = lax.div(qi * tq, bk)
    n_diag = max(1, tq // bk)

    q_both = jnp.concatenate(qs, axis=0)
    row2 = jnp.concatenate([row, row], axis=0)

    def block(kb, carry, masked):
        m, l, acc = carry
        k0 = pl.multiple_of(kb * bk, bk)
        s = _dot_nt(q_both, k_ref[0, pl.ds(k0, bk), :])
        if kind == "fox":
            c2 = [jnp.broadcast_to(c_ref[0, pl.ds(2 * pair + e, 1), pl.ds(k0, bk)], (tq, bk)) for e in range(2)]
            s = s - jnp.concatenate(c2, axis=0)
        if masked:
            valid = (k0 + lax.broadcasted_iota(I32, (1, bk), 1)) <= row2
            s = jnp.where(valid, s, -jnp.inf)
        m_new = jnp.maximum(m, jnp.max(s, axis=1, keepdims=True))
        alpha = jnp.exp2(m - m_new)
        p = jnp.exp2(s - m_new)
        l = alpha * l + jnp.sum(p, axis=1, keepdims=True)
        acc = alpha * acc + _dot(p.astype(BF16), v_ref[0, pl.ds(k0, bk), :])
        return m_new, l, acc

    carry = (jnp.full((2 * tq, 1), -jnp.inf, F32), jnp.zeros((2 * tq, 1), F32), jnp.zeros((2 * tq, LANES), F32))
    carry = lax.fori_loop(0, n_full, lambda kb, c: block(kb, c, False), carry)
    for d in range(n_diag):
        carry = block(n_full + d, carry, True)
    out = carry[2] / carry[1]
    lane = lax.broadcasted_iota(I32, (tq, LANES), 1)
    o_ref[0] = jnp.where(lane < HEAD_DIM, out[:tq], out[tq:])


def _sb_attn_kernel(bk, q_ref, k_ref, v_ref, o_ref):
    tq, wq = q_ref.shape[1], q_ref.shape[2]
    pair, qi = pl.program_id(1), pl.program_id(2)
    q2 = q_ref[0]
    zero = jnp.zeros_like(q2)
    qs = [jnp.where(m, q2, zero) for m in _head_masks("sb", wq, pair)]
    row = qi * tq + lax.broadcasted_iota(I32, (tq, 1), 0)
    n_full = lax.div(qi * tq, bk)
    n_diag = max(1, tq // bk)
    jj = lax.broadcasted_iota(I32, (bk, bk), 0)
    ss = lax.broadcasted_iota(I32, (bk, bk), 1)
    later_mat = jnp.where(jj > ss, 1.0, 0.0).astype(BF16)

    q_both = jnp.concatenate(qs, axis=0)
    row2 = jnp.concatenate([row, row], axis=0)

    def block(kb, carry, masked):
        run, acc = carry
        k0 = pl.multiple_of(kb * bk, bk)
        z = _dot_nt(q_both, k_ref[0, pl.ds(k0, bk), :])
        drop = jnp.maximum(z, 0.0) + jnp.log2(1.0 + jnp.exp2(-jnp.abs(z)))
        logw = z - drop
        if masked:
            valid = (k0 + lax.broadcasted_iota(I32, (1, bk), 1)) < row2
            drop = jnp.where(valid, drop, 0.0)
        hi = drop.astype(BF16)
        lo = (drop - hi.astype(F32)).astype(BF16)
        parts = _dot(jnp.concatenate([hi, lo], axis=0), later_mat)
        inside = parts[:2 * tq] + parts[2 * tq:]
        w = jnp.exp2(logw - (inside + run))
        if masked:
            w = jnp.where(valid, w, 0.0)
        acc = acc + _dot(w.astype(BF16), v_ref[0, pl.ds(k0, bk), :])
        run = run + (inside[:, 0:1] + drop[:, 0:1])
        return run, acc

    carry = (jnp.zeros((2 * tq, 1), F32), jnp.zeros((2 * tq, LANES), F32))
    for d in reversed(range(n_diag)):
        carry = block(n_full + d, carry, True)
    assert n_diag % 2 == 0

    def two_blocks(it, c):
        kb = n_full - 1 - 2 * it
        return block(kb - 1, block(kb, c, False), False)

    carry = lax.fori_loop(0, lax.div(n_full, 2), two_blocks, carry)
    lane = lax.broadcasted_iota(I32, (tq, LANES), 1)
    o_ref[0] = jnp.where(lane < HEAD_DIM, carry[1][:tq], carry[1][tq:])


def _pair_attention(kind, q, k, v, c=None, *, tq, bk):
    b, s, wtot = q.shape
    npairs = v.shape[2] // LANES
    wq = wtot // npairs
    in_specs = [pl.BlockSpec((1, tq, wq), lambda bb, p, i: (bb, i, p)),
                pl.BlockSpec((1, s, wq), lambda bb, p, i: (bb, 0, p)),
                pl.BlockSpec((1, s, LANES), lambda bb, p, i: (bb, 0, p))]
    args = [q, k, v]
    if kind == "fox":
        in_specs.append(pl.BlockSpec((1, 8, s), lambda bb, p, i: (bb, 0, 0)))
        args.append(c)
    body = functools.partial(_sb_attn_kernel, bk) if kind == "sb" else functools.partial(_softmax_attn_kernel, kind, bk)
    return pl.pallas_call(
        body,
        grid=(b, npairs, s // tq),
        in_specs=in_specs,
        out_specs=pl.BlockSpec((1, tq, LANES), lambda bb, p, i: (bb, i, p)),
        out_shape=jax.ShapeDtypeStruct((b, s, npairs * LANES), F32),
        compiler_params=_cparams(("arbitrary",) * 3, 40),
        name=kind + "_attention",
    )(*args)


def _outproj_kernel(x_ref, yf_ref, ym_ref, ys_ref, gf_ref, gm_ref, gs_ref, wf_ref, wm_ref, ws_ref, o_ref):
    y = _dot(_rms(yf_ref[...], gf_ref[...]).astype(BF16), wf_ref[...])
    y += _dot(_rms(ym_ref[...], gm_ref[...]).astype(BF16), wm_ref[...])
    y += _dot(_rms(ys_ref[...], gs_ref[...]).astype(BF16), ws_ref[...])
    o_ref[...] = x_ref[...] + y


def _outproj(x, yf, ym, ys, gf, gm, gs, wf, wm, ws, *, ts):
    t, d = x.shape
    full = lambda a: pl.BlockSpec(a.shape, lambda i: (0,) * a.ndim)
    row = lambda a: pl.BlockSpec((ts, a.shape[1]), lambda i: (i, 0))
    return pl.pallas_call(
        _outproj_kernel,
        grid=(t // ts,),
        in_specs=[row(x), row(yf), row(ym), row(ys), full(gf), full(gm), full(gs), full(wf), full(wm), full(ws)],
        out_specs=row(x),
        out_shape=jax.ShapeDtypeStruct((t, d), F32),
        compiler_params=_cparams(("arbitrary",), 40),
        name="outproj",
    )(x, yf, ym, ys, gf, gm, gs, wf, wm, ws)


def _extract_topk(items):
    def body(i, c):
        for work_ref, row_ids, val_ref, idx_ref in items:
            s = work_ref[...]
            m = jnp.max(s, axis=0, keepdims=True)
            idx = jnp.min(jnp.where(s == m, row_ids, _ID_SENTINEL), axis=0, keepdims=True)
            work_ref[...] = jnp.where(row_ids == idx, -jnp.inf, s)
            val_ref[pl.ds(i, 1), :] = m
            idx_ref[pl.ds(i, 1), :] = idx
        return c

    lax.fori_loop(0, PEER_TOPK, body, 0)


_ID_SENTINEL = float(PEER_TOPK * PEER_TOPK)
_CAND_GROUPS = [(0, 0), (0, 8)] + [(j, 0) for j in range(1, 8)]
_N_CAND_ROWS = 8 * (len(_CAND_GROUPS) + 1)


def _candidate_cells(tr):
    r = lax.broadcasted_iota(I32, (_N_CAND_ROWS, tr), 0)
    grp, i = lax.shift_right_logical(r, 3), jnp.bitwise_and(r, 7)
    last = len(_CAND_GROUPS)
    j = jnp.where(grp <= 1, 0, jnp.where(grp < last, grp - 1, 8 + i))
    k = jnp.where(grp == 1, 8 + i, jnp.where(grp == last, 0, i))
    return (j * PEER_TOPK + k).astype(F32)


def _route_kernel(x_ref, g_ref, wq_ref, k1_ref, k2_ref, h_ref, ids_ref, gates_ref,
                  q_s, work1, work2, top_s, idx_s, best_v, best_p, ids_t, gates_t):
    k = PEER_TOPK
    tr = x_ref.shape[0]
    hb = _rms(x_ref[...], g_ref[...]).astype(BF16)
    h_ref[...] = hb
    q = _dot(hb, wq_ref[...]).astype(BF16)
    for i in range(2 * PEER_HEADS):
        q_s[i] = q[:, i * PEER_HALF:(i + 1) * PEER_HALF]
    key_rows = lax.broadcasted_iota(I32, (PEER_N_KEYS, tr), 0).astype(F32)
    cand_id = _candidate_cells(tr)
    row8 = lax.broadcasted_iota(I32, (8, tr), 0)

    def head(h, c):
        for half, key_ref in enumerate((k1_ref, k2_ref)):
            work1[half] = _dot_nt(key_ref[...], q_s[2 * h + half])
        _extract_topk([(work1.at[half], key_rows, top_s.at[half], idx_s.at[half]) for half in range(2)])
        for g, (j, k0) in enumerate(_CAND_GROUPS):
            sums = top_s[0, j:j + 1, :] + top_s[1, k0:k0 + 8, :]
            n_ok = PEER_TOPK // (j + 1) - k0
            work2[8 * g:8 * g + 8, :] = sums if n_ok >= 8 else jnp.where(row8 < n_ok, sums, -jnp.inf)
        work2[_N_CAND_ROWS - 8:, :] = top_s[0, 8:16, :] + top_s[1, 0:1, :]
        _extract_topk([(work2, cand_id, best_v, best_p)])
        pos = best_p[...].astype(I32)
        pj, pk = lax.shift_right_logical(pos, 4), jnp.bitwise_and(pos, k - 1)
        a = jnp.zeros(pos.shape, F32)
        b = jnp.zeros(pos.shape, F32)
        for j in range(k):
            a = jnp.where(pj == j, idx_s[0, j:j + 1, :], a)
            b = jnp.where(pk == j, idx_s[1, j:j + 1, :], b)
        best = best_v[...]
        e = jnp.exp(best - best[0:1, :])
        r0 = pl.multiple_of(h * k, k)
        ids_t[pl.ds(r0, k), :] = a * PEER_N_KEYS + b
        gates_t[pl.ds(r0, k), :] = e / jnp.sum(e, axis=0, keepdims=True)
        return c

    lax.fori_loop(0, PEER_HEADS, head, 0)
    ids_ref[...] = ids_t[...].T.astype(I32)
    gates_ref[...] = gates_t[...].T


def _peer_route(x, g, wq, k1, k2, *, tr):
    t, d = x.shape
    full = lambda a: pl.BlockSpec(a.shape, lambda i: (0,) * a.ndim)
    row = lambda w: pl.BlockSpec((tr, w), lambda i: (i, 0))
    k = PEER_TOPK
    return pl.pallas_call(
        _route_kernel,
        grid=(t // tr,),
        in_specs=[row(d), full(g), full(wq), full(k1), full(k2)],
        out_specs=[row(d), row(LANES), row(LANES)],
        out_shape=[jax.ShapeDtypeStruct((t, d), BF16), jax.ShapeDtypeStruct((t, LANES), I32),
                   jax.ShapeDtypeStruct((t, LANES), F32)],
        scratch_shapes=[pltpu.VMEM((2 * PEER_HEADS, tr, PEER_HALF), BF16),
                        pltpu.VMEM((2, PEER_N_KEYS, tr), F32), pltpu.VMEM((_N_CAND_ROWS, tr), F32),
                        pltpu.VMEM((2, k, tr), F32), pltpu.VMEM((2, k, tr), F32),
                        pltpu.VMEM((k, tr), F32), pltpu.VMEM((k, tr), F32),
                        pltpu.VMEM((PEER_HEADS * k, tr), F32), pltpu.VMEM((PEER_HEADS * k, tr), F32)],
        compiler_params=_cparams(("arbitrary",), 40),
        name="peer_route",
    )(x, g, wq, k1, k2)


def _gelu(x):
    return 0.5 * x * (1.0 + lax.erf(x * np.float32(np.sqrt(0.5))))


def _peer_kernel(final_norm, n_blocks, h_ref, ids_ref, gates_ref, ids_next_ref, gates_next_ref, u_ref, v_ref,
                 x_ref, fg_ref, o_ref, gate_s, a_s, acc_s):
    tq = h_ref.shape[0]
    nb = u_ref.shape[0]
    d = v_ref.shape[1]
    n_chunks = nb // PEER_CHUNK
    i, j = pl.program_id(0), pl.program_id(1)
    last_tile = pl.num_programs(0) - 1
    key = lax.broadcasted_iota(I32, (PEER_N_KEYS, LANES), 0)

    def build_gates(ids_r, gates_r, t0):
        ids = ids_r[pl.ds(t0, GATE_UNROLL), :]
        gate = gates_r[pl.ds(t0, GATE_UNROLL), :]
        i1 = lax.shift_right_logical(ids, 7)
        i2 = jnp.bitwise_and(ids, PEER_N_KEYS - 1)
        for k in range(GATE_UNROLL):
            p = jnp.where(i1[k:k + 1] == key, gate[k:k + 1], 0.0).astype(BF16)
            q = jnp.where(i2[k:k + 1] == key, 1.0, 0.0).T.astype(BF16)
            g = _dot(p, q).astype(BF16)
            row = (t0 + k) * GATE_PITCH
            row = row if isinstance(row, int) else pl.multiple_of(row, 8)
            gate_s[pl.ds(row, PEER_N_KEYS // 2), :] = pltpu.bitcast(g, jnp.uint32)

    def activate(block, dst):
        for c in range(n_chunks):
            cols = slice(c * PEER_CHUNK, (c + 1) * PEER_CHUNK)
            act = _dot_nt(h_ref[...], u_ref[cols, :])
            word = gate_s[pl.ds(block * n_chunks + c, tq, stride=GATE_PITCH), :]
            g_even = pltpu.bitcast(lax.shift_left(word, jnp.uint32(16)), F32)
            g_odd = pltpu.bitcast(jnp.bitwise_and(word, jnp.uint32(0xFFFF0000)), F32)
            a = jnp.concatenate([_gelu(act[:, :LANES]) * g_even, _gelu(act[:, LANES:]) * g_odd], axis=1)
            dst[:, cols] = a.astype(BF16)

    def mix(src):
        acc_s[...] += _dot(src[...], v_ref[...])

    @pl.when((i == 0) & (j == 0))
    def _():
        def build(grp, c):
            build_gates(ids_ref, gates_ref, pl.multiple_of(grp * GATE_UNROLL, GATE_UNROLL))
            return c

        lax.fori_loop(0, tq // GATE_UNROLL, build, 0)

    @pl.when(j == 0)
    def _():
        acc_s[...] = jnp.zeros_like(acc_s)
        activate(0, a_s.at[0])

    for parity in range(2):
        @pl.when((j >= 1) & (j < n_blocks) & (lax.rem(j, 2) == parity))
        def _():
            mix(a_s.at[1 - parity])
            activate(j, a_s.at[parity])

    def finalize():
        y = x_ref[...] + acc_s[...]
        o_ref[...] = _rms(y, fg_ref[...]) if final_norm else y

    last_src = a_s.at[(n_blocks - 1) % 2]

    @pl.when((j == n_blocks) & (i < last_tile))
    def _():
        n_parts = d // PEER_CHUNK
        for part in range(n_parts):
            cols = slice(part * PEER_CHUNK, (part + 1) * PEER_CHUNK)
            acc_s[:, cols] += _dot(last_src[...], v_ref[:, cols])
            for t0 in range(part * tq // n_parts, (part + 1) * tq // n_parts, GATE_UNROLL):
                build_gates(ids_next_ref, gates_next_ref, t0)
        finalize()

    @pl.when((j == n_blocks) & (i == last_tile))
    def _():
        mix(last_src)
        finalize()


def _peer_main(h, ids, gates, u, v, x, fg, *, final_norm, tq, nb):
    t, d = x.shape
    n_blocks = u.shape[0] // nb
    n_tiles = t // tq
    tok = lambda w: pl.BlockSpec((tq, w), lambda i, j: (i, 0))
    tok_next = pl.BlockSpec((tq, LANES), lambda i, j: (jnp.minimum(i + 1, n_tiles - 1), 0))
    return pl.pallas_call(
        functools.partial(_peer_kernel, final_norm, n_blocks),
        grid=(n_tiles, n_blocks + 1),
        in_specs=[tok(d), tok(LANES), tok(LANES), tok_next, tok_next,
                  pl.BlockSpec((nb, d), lambda i, j: (jnp.minimum(j, n_blocks - 1), 0)),
                  pl.BlockSpec((nb, d), lambda i, j: (jnp.maximum(j - 1, 0), 0)),
                  tok(d), pl.BlockSpec((1, d), lambda i, j: (0, 0))],
        out_specs=tok(d),
        out_shape=jax.ShapeDtypeStruct((t, d), F32),
        scratch_shapes=[pltpu.VMEM((tq * GATE_PITCH, LANES), jnp.uint32), pltpu.VMEM((2, tq, nb), BF16),
                        pltpu.VMEM((tq, d), F32)],
        compiler_params=_cparams(("arbitrary", "arbitrary"), 58),
        name="peer_main",
    )(h, ids, gates, ids, gates, u, v, x, fg)


def _prep_w_in(w_in, fgate_b):
    d = w_in.shape[0]
    o = np.cumsum([0, 384, 384, 384, FOX_HEADS, MLA_Q_RANK, MLA_KV_RANK, MLA_ROPE_DIM, 384, 384, 384])
    fq, fk, fv, fl, cq, ckv, kr, sq, sk, sv = [w_in[:, o[i]:o[i + 1]] for i in range(10)]
    half = MLA_ROPE_DIM // 2
    kr4 = jnp.concatenate([kr[:, :half]] * MLA_HEADS + [kr[:, half:]] * MLA_HEADS, axis=1)
    fl_pad = jnp.pad(fl, ((0, 0), (0, LANES - FOX_HEADS)))
    w_all = jnp.concatenate([fq, fk, fv, sq, sk, sv, cq, ckv, fl_pad, kr4], axis=1).astype(BF16)
    assert w_all.shape == (d, _IN_COLS_PADDED)
    fb = jnp.pad(fgate_b, (0, LANES - FOX_HEADS)).reshape(1, LANES)
    return w_all, fb


def _prep_w_uq(w_uq):
    per = MLA_NOPE_DIM + MLA_ROPE_DIM
    half = MLA_ROPE_DIM // 2
    nope = [w_uq[:, h * per:h * per + MLA_NOPE_DIM] for h in range(MLA_HEADS)]
    x1 = [w_uq[:, h * per + MLA_NOPE_DIM:h * per + MLA_NOPE_DIM + half] for h in range(MLA_HEADS)]
    x2 = [w_uq[:, h * per + MLA_NOPE_DIM + half:(h + 1) * per] for h in range(MLA_HEADS)]
    return jnp.concatenate(nope + x1 + x2, axis=1).astype(BF16)


def _prep_w_ukv(w_ukv):
    per = MLA_NOPE_DIM + MLA_V_DIM
    kn = [w_ukv[:, h * per:h * per + MLA_NOPE_DIM] for h in range(MLA_HEADS)]
    vv = [w_ukv[:, h * per + MLA_NOPE_DIM:(h + 1) * per] for h in range(MLA_HEADS)]
    return jnp.concatenate(kn + vv, axis=1).astype(BF16)


def _rope_lane_freq():
    half = MLA_ROPE_DIM // 2
    inv_freq = 1.0 / (ROPE_THETA ** (jnp.arange(half, dtype=F32) / half))
    return jnp.tile(inv_freq, LANES // half).reshape(1, LANES)


def kernel(x, attn_norm_g, w_in, fgate_b, mla_q_norm_g, w_uq, mla_kv_norm_g, w_ukv, out_norm_fox, out_norm_mla, out_norm_sb, w_out, ffn_norm_g, peer_w_query, peer_keys1, peer_keys2, peer_u, peer_v, final_norm_g):
    b, s, d = x.shape
    depth = w_in.shape[0]
    t = b * s
    xt = x.reshape(t, d)
    freq = _rope_lane_freq()
    row = lambda a: a.reshape(1, -1)
    for l in range(depth):
        w_all, fb = _prep_w_in(w_in[l], fgate_b[l])
        fq, fk, fv, sq, sk, sv, logf, mq, mk, mv = _inproj(
            xt, row(attn_norm_g[l]), w_all, fb, row(mla_q_norm_g[l]), _prep_w_uq(w_uq[l]),
            row(mla_kv_norm_g[l]), _prep_w_ukv(w_ukv[l]), freq, seq=s, ts=512)
        bs = lambda a: a.reshape(b, s, a.shape[1])
        c = _fox_cumsum(bs(logf))
        y_fox = _pair_attention("fox", bs(fq), bs(fk), bs(fv), c, tq=512, bk=512)
        y_mla = _pair_attention("mla", bs(mq), bs(mk), bs(mv), tq=512, bk=512)
        y_sb = _pair_attention("sb", bs(sq), bs(sk), bs(sv), tq=512, bk=256)
        wo = w_out[l].astype(BF16)
        xt = _outproj(xt, y_fox.reshape(t, -1), y_mla.reshape(t, -1), y_sb.reshape(t, -1),
                      row(out_norm_fox[l]), row(out_norm_mla[l]), row(out_norm_sb[l]),
                      wo[0:384], wo[384:640], wo[640:1024], ts=512)
        h2, ids, gates = _peer_route(xt, row(ffn_norm_g[l]), peer_w_query[l].astype(BF16),
                                     peer_keys1[l].astype(BF16), peer_keys2[l].astype(BF16), tr=512)
        xt = _peer_main(h2, ids, gates, peer_u[l].astype(BF16), peer_v[l].astype(BF16), xt,
                        row(final_norm_g), final_norm=(l == depth - 1), tq=512, nb=2048)
    return xt.reshape(b, s, d)
```

```python
import functools

import jax
import jax.numpy as jnp
import numpy as np
from jax import lax
from jax.experimental import pallas as pl
from jax.experimental.pallas import tpu as pltpu

F32, BF16, I32 = jnp.float32, jnp.bfloat16, jnp.int32

NORM_EPS = 1e-6
HEAD_DIM = 64
FOX_HEADS = 6
SB_HEADS = 6
MLA_HEADS = 4
MLA_Q_RANK = 256
MLA_KV_RANK = 128
MLA_NOPE_DIM = 64
MLA_ROPE_DIM = 32
MLA_V_DIM = 64
ROPE_THETA = 10000.0
PEER_HEADS = 8
PEER_N_KEYS = 128
PEER_TOPK = 16
PEER_HALF = 128

LOG2E = float(np.log2(np.e))
LANES = 128
MIB = 1024 * 1024
GATE_PITCH = 72
GATE_UNROLL = 16
PEER_CHUNK = 256

TOKEN_TILE = 512
ROUTE_TILE = 1024
KEY_BLOCK = {"fox": 512, "mla": 512, "sb": 256}
EXPERT_BLOCK = 2048
VMEM_MIB = {"inproj": 52, "fox_cumsum": 16, "attention": 40, "outproj": 40, "peer_route": 48, "peer_main": 58}


def _cparams(sem, vmem_mib):
    return pltpu.CompilerParams(dimension_semantics=sem, vmem_limit_bytes=vmem_mib * MIB)


def _rms(x, g):
    return x * lax.rsqrt(jnp.mean(x * x, axis=-1, keepdims=True) + NORM_EPS) * g


def _log_sigmoid(y):
    return jnp.minimum(y, 0.0) - jnp.log1p(jnp.exp(-jnp.abs(y)))


def _dot(a, b):
    return jnp.dot(a, b, preferred_element_type=F32)


def _dot_nt(a, b):
    return lax.dot_general(a, b, (((1,), (1,)), ((), ())), preferred_element_type=F32)


def _dot_tn(a, b):
    return lax.dot_general(a, b, (((0,), (0,)), ((), ())), preferred_element_type=F32)


_W384 = FOX_HEADS * HEAD_DIM
_W_MLA = MLA_HEADS * MLA_V_DIM
_OFF_FQ, _OFF_FK, _OFF_FV = 0, 384, 768
_OFF_SQ, _OFF_SK, _OFF_SV = 1152, 1536, 1920
_OFF_CQ, _OFF_CKV, _OFF_FLOG, _OFF_KR = 2304, 2560, 2688, 2816
_IN_COLS_PADDED = 2944


def _inproj_kernel(seq, x_ref, g_ref, w_ref, fb_ref, qg_ref, wuq_ref, kvg_ref, wukv_ref, freq_ref,
                   fq_ref, fk_ref, fv_ref, sq_ref, sk_ref, sv_ref, logf_ref, mq_ref, mk_ref, mv_ref):
    ts = x_ref.shape[0]
    h = _rms(x_ref[...], g_ref[...]).astype(BF16)
    p = _dot(h, w_ref[...])
    qk_scale = HEAD_DIM ** -0.5 * LOG2E
    fq_ref[...] = (p[:, _OFF_FQ:_OFF_FQ + 384] * qk_scale).astype(BF16)
    fk_ref[...] = p[:, _OFF_FK:_OFF_FK + 384].astype(BF16)
    fv_ref[...] = p[:, _OFF_FV:_OFF_FV + 384].astype(BF16)
    sq_ref[...] = (p[:, _OFF_SQ:_OFF_SQ + 384] * qk_scale).astype(BF16)
    sk_ref[...] = p[:, _OFF_SK:_OFF_SK + 384].astype(BF16)
    sv_ref[...] = p[:, _OFF_SV:_OFF_SV + 384].astype(BF16)
    logf_ref[...] = _log_sigmoid(p[:, _OFF_FLOG:_OFF_FLOG + LANES] + fb_ref[...])

    mla_scale = (MLA_NOPE_DIM + MLA_ROPE_DIM) ** -0.5 * LOG2E
    cq = _rms(p[:, _OFF_CQ:_OFF_CQ + MLA_Q_RANK], qg_ref[...]).astype(BF16)
    qup = _dot(cq, wuq_ref[...])
    ckv = _rms(p[:, _OFF_CKV:_OFF_CKV + MLA_KV_RANK], kvg_ref[...]).astype(BF16)
    kvup = _dot(ckv, wukv_ref[...])

    pos0 = lax.rem(pl.program_id(0) * ts, seq)
    pos = (pos0 + lax.broadcasted_iota(I32, (ts, LANES), 0)).astype(F32)
    ang = pos * freq_ref[...]
    cos, sin = jnp.cos(ang), jnp.sin(ang)
    lane = lax.broadcasted_iota(I32, (ts, LANES), 1)
    sin_signed = jnp.where(lane < LANES // 2, -sin, sin)

    def rope(x):
        return x * cos + pltpu.roll(x, LANES // 2, 1) * sin_signed

    qr = (rope(qup[:, 256:384]) * mla_scale).astype(BF16)
    kr = rope(p[:, _OFF_KR:_OFF_KR + LANES]).astype(BF16)
    mq_ref[:, 0:128] = (qup[:, 0:128] * mla_scale).astype(BF16)
    mq_ref[:, 128:256] = qr
    mq_ref[:, 256:384] = (qup[:, 128:256] * mla_scale).astype(BF16)
    mq_ref[:, 384:512] = qr
    mk_ref[:, 0:128] = kvup[:, 0:128].astype(BF16)
    mk_ref[:, 128:256] = kr
    mk_ref[:, 256:384] = kvup[:, 128:256].astype(BF16)
    mk_ref[:, 384:512] = kr
    mv_ref[...] = kvup[:, 256:512].astype(BF16)


def _inproj(x, g, w_all, fb, qg, wuq, kvg, wukv, freq, *, seq, ts):
    t, d = x.shape
    full = lambda a: pl.BlockSpec(a.shape, lambda i: (0,) * a.ndim)
    row = lambda w: pl.BlockSpec((ts, w), lambda i: (i, 0))
    outs = ([jax.ShapeDtypeStruct((t, 384), BF16)] * 6 + [jax.ShapeDtypeStruct((t, LANES), F32)]
            + [jax.ShapeDtypeStruct((t, 512), BF16)] * 2 + [jax.ShapeDtypeStruct((t, 256), BF16)])
    return pl.pallas_call(
        functools.partial(_inproj_kernel, seq),
        grid=(t // ts,),
        in_specs=[row(d), full(g), full(w_all), full(fb), full(qg), full(wuq), full(kvg), full(wukv), full(freq)],
        out_specs=[row(384)] * 6 + [row(LANES)] + [row(512)] * 2 + [row(256)],
        out_shape=outs,
        compiler_params=_cparams(("arbitrary",), VMEM_MIB["inproj"]),
        name="inproj",
    )(x, g, w_all, fb, qg, wuq, kvg, wukv, freq)


def _split3(x):
    hi = x.astype(BF16)
    r = x - hi.astype(F32)
    mid = r.astype(BF16)
    lo = (r - mid.astype(F32)).astype(BF16)
    return hi, mid, lo


def _cumsum_kernel(logf_ref, c_ref):
    seq = logf_ref.shape[1]
    r = lax.broadcasted_iota(I32, (LANES, LANES), 0)
    c = lax.broadcasted_iota(I32, (LANES, LANES), 1)
    tri = jnp.where(r <= c, 1.0, 0.0).astype(BF16)
    carry = jnp.zeros((LANES, 1), F32)
    for i in range(seq // LANES):
        chunk = logf_ref[0, i * LANES:(i + 1) * LANES, :]
        hi, mid, lo = _split3(chunk)
        cs = _dot_tn(hi, tri) + _dot_tn(mid, tri) + _dot_tn(lo, tri) + carry
        c_ref[0, :, i * LANES:(i + 1) * LANES] = cs[0:8, :] * LOG2E
        carry = cs[:, LANES - 1:LANES]


def _fox_cumsum(logf):
    b, s, _ = logf.shape
    return pl.pallas_call(
        _cumsum_kernel,
        grid=(b,),
        in_specs=[pl.BlockSpec((1, s, LANES), lambda i: (i, 0, 0))],
        out_specs=pl.BlockSpec((1, 8, s), lambda i: (i, 0, 0)),
        out_shape=jax.ShapeDtypeStruct((b, 8, s), F32),
        compiler_params=_cparams(("arbitrary",), VMEM_MIB["fox_cumsum"]),
        name="fox_cumsum",
    )(logf)


def _head_masks(kind, wq, pair):
    lane = lax.broadcasted_iota(I32, (1, wq), 1)
    masks = []
    for e in range(2):
        m = (lane >= HEAD_DIM * e) & (lane < HEAD_DIM * (e + 1))
        if kind == "mla":
            head = 2 * pair + e
            rl = jnp.bitwise_and(lane - LANES, LANES // 2 - 1)
            m = m | ((lane >= LANES) & (rl >= 16 * head) & (rl < 16 * (head + 1)))
        masks.append(m)
    return masks


def _softmax_attn_kernel(kind, bk, *refs):
    if kind == "fox":
        q_ref, k_ref, v_ref, c_ref, o_ref = refs
    else:
        q_ref, k_ref, v_ref, o_ref = refs
    tq, wq = q_ref.shape[1], q_ref.shape[2]
    pair, qi = pl.program_id(1), pl.program_id(2)
    q2 = q_ref[0]
    zero = jnp.zeros_like(q2)
    qs = [jnp.where(m, q2, zero) for m in _head_masks(kind, wq, pair)]
    row = qi * tq + lax.broadcasted_iota(I32, (tq, 1), 0)
    n_full = lax.div(qi * tq, bk)
    n_diag = max(1, tq // bk)

    q_both = jnp.concatenate(qs, axis=0)
    row2 = jnp.concatenate([row, row], axis=0)

    def block(kb, carry, masked):
        m, l, acc = carry
        k0 = pl.multiple_of(kb * bk, bk)
        s = _dot_nt(q_both, k_ref[0, pl.ds(k0, bk), :])
        if kind == "fox":
            c2 = [jnp.broadcast_to(c_ref[0, pl.ds(2 * pair + e, 1), pl.ds(k0, bk)], (tq, bk)) for e in range(2)]
            s = s - jnp.concatenate(c2, axis=0)
        if masked:
            valid = (k0 + lax.broadcasted_iota(I32, (1, bk), 1)) <= row2
            s = jnp.where(valid, s, -jnp.inf)
        m_new = jnp.maximum(m, jnp.max(s, axis=1, keepdims=True))
        alpha = jnp.exp2(m - m_new)
        p = jnp.exp2(s - m_new)
        l = alpha * l + jnp.sum(p, axis=1, keepdims=True)
        acc = alpha * acc + _dot(p.astype(BF16), v_ref[0, pl.ds(k0, bk), :])
        return m_new, l, acc

    carry = (jnp.full((2 * tq, 1), -jnp.inf, F32), jnp.zeros((2 * tq, 1), F32), jnp.zeros((2 * tq, LANES), F32))
    carry = lax.fori_loop(0, n_full, lambda kb, c: block(kb, c, False), carry)
    for d in range(n_diag):
        carry = block(n_full + d, carry, True)
    out = carry[2] / carry[1]
    lane = lax.broadcasted_iota(I32, (tq, LANES), 1)
    o_ref[0] = jnp.where(lane < HEAD_DIM, out[:tq], out[tq:])


def _sb_attn_kernel(bk, q_ref, k_ref, v_ref, o_ref):
    tq, wq = q_ref.shape[1], q_ref.shape[2]
    pair, qi = pl.program_id(1), pl.program_id(2)
    q2 = q_ref[0]
    zero = jnp.zeros_like(q2)
    qs = [jnp.where(m, q2, zero) for m in _head_masks("sb", wq, pair)]
    row = qi * tq + lax.broadcasted_iota(I32, (tq, 1), 0)
    n_full = lax.div(qi * tq, bk)
    n_diag = max(1, tq // bk)
    jj = lax.broadcasted_iota(I32, (bk, bk), 0)
    ss = lax.broadcasted_iota(I32, (bk, bk), 1)
    later_mat = jnp.where(jj > ss, 1.0, 0.0).astype(BF16)

    q_both = jnp.concatenate(qs, axis=0)
    row2 = jnp.concatenate([row, row], axis=0)

    def block(kb, carry, masked):
        run, acc = carry
        k0 = pl.multiple_of(kb * bk, bk)
        z = _dot_nt(q_both, k_ref[0, pl.ds(k0, bk), :])
        drop = jnp.maximum(z, 0.0) + jnp.log2(1.0 + jnp.exp2(-jnp.abs(z)))
        logw = z - drop
        if masked:
            valid = (k0 + lax.broadcasted_iota(I32, (1, bk), 1)) < row2
            drop = jnp.where(valid, drop, 0.0)
        hi = drop.astype(BF16)
        lo = (drop - hi.astype(F32)).astype(BF16)
        parts = _dot(jnp.concatenate([hi, lo], axis=0), later_mat)
        inside = parts[:2 * tq] + parts[2 * tq:]
        w = jnp.exp2(logw - (inside + run))
        if masked:
            w = jnp.where(valid, w, 0.0)
        acc = acc + _dot(w.astype(BF16), v_ref[0, pl.ds(k0, bk), :])
        run = run + (inside[:, 0:1] + drop[:, 0:1])
        return run, acc

    carry = (jnp.zeros((2 * tq, 1), F32), jnp.zeros((2 * tq, LANES), F32))
    for d in reversed(range(n_diag)):
        carry = block(n_full + d, carry, True)
    assert n_diag % 2 == 0

    def two_blocks(it, c):
        kb = n_full - 1 - 2 * it
        return block(kb - 1, block(kb, c, False), False)

    carry = lax.fori_loop(0, lax.div(n_full, 2), two_blocks, carry)
    lane = lax.broadcasted_iota(I32, (tq, LANES), 1)
    o_ref[0] = jnp.where(lane < HEAD_DIM, carry[1][:tq], carry[1][tq:])


def _pair_attention(kind, q, k, v, c=None, *, tq, bk):
    b, s, wtot = q.shape
    npairs = v.shape[2] // LANES
    wq = wtot // npairs
    in_specs = [pl.BlockSpec((1, tq, wq), lambda bb, p, i: (bb, i, p)),
                pl.BlockSpec((1, s, wq), lambda bb, p, i: (bb, 0, p)),
                pl.BlockSpec((1, s, LANES), lambda bb, p, i: (bb, 0, p))]
    args = [q, k, v]
    if kind == "fox":
        in_specs.append(pl.BlockSpec((1, 8, s), lambda bb, p, i: (bb, 0, 0)))
        args.append(c)
    body = functools.partial(_sb_attn_kernel, bk) if kind == "sb" else functools.partial(_softmax_attn_kernel, kind, bk)
    return pl.pallas_call(
        body,
        grid=(b, npairs, s // tq),
        in_specs=in_specs,
        out_specs=pl.BlockSpec((1, tq, LANES), lambda bb, p, i: (bb, i, p)),
        out_shape=jax.ShapeDtypeStruct((b, s, npairs * LANES), F32),
        compiler_params=_cparams(("arbitrary",) * 3, VMEM_MIB["attention"]),
        name=kind + "_attention",
    )(*args)


def _outproj_kernel(x_ref, yf_ref, ym_ref, ys_ref, gf_ref, gm_ref, gs_ref, wf_ref, wm_ref, ws_ref, o_ref):
    y = _dot(_rms(yf_ref[...], gf_ref[...]).astype(BF16), wf_ref[...])
    y += _dot(_rms(ym_ref[...], gm_ref[...]).astype(BF16), wm_ref[...])
    y += _dot(_rms(ys_ref[...], gs_ref[...]).astype(BF16), ws_ref[...])
    o_ref[...] = x_ref[...] + y


def _outproj(x, yf, ym, ys, gf, gm, gs, wf, wm, ws, *, ts):
    t, d = x.shape
    full = lambda a: pl.BlockSpec(a.shape, lambda i: (0,) * a.ndim)
    row = lambda a: pl.BlockSpec((ts, a.shape[1]), lambda i: (i, 0))
    return pl.pallas_call(
        _outproj_kernel,
        grid=(t // ts,),
        in_specs=[row(x), row(yf), row(ym), row(ys), full(gf), full(gm), full(gs), full(wf), full(wm), full(ws)],
        out_specs=row(x),
        out_shape=jax.ShapeDtypeStruct((t, d), F32),
        compiler_params=_cparams(("arbitrary",), VMEM_MIB["outproj"]),
        name="outproj",
    )(x, yf, ym, ys, gf, gm, gs, wf, wm, ws)


def _extract_topk(items):
    def body(i, c):
        for work_ref, row_ids, val_ref, idx_ref in items:
            s = work_ref[...]
            m = jnp.max(s, axis=0, keepdims=True)
            idx = jnp.min(jnp.where(s == m, row_ids, _ID_SENTINEL), axis=0, keepdims=True)
            work_ref[...] = jnp.where(row_ids == idx, -jnp.inf, s)
            val_ref[pl.ds(i, 1), :] = m
            idx_ref[pl.ds(i, 1), :] = idx
        return c

    lax.fori_loop(0, PEER_TOPK, body, 0)


_ID_SENTINEL = float(PEER_TOPK * PEER_TOPK)
_CAND_GROUPS = [(0, 0), (0, 8)] + [(j, 0) for j in range(1, 8)]
_N_CAND_ROWS = 8 * (len(_CAND_GROUPS) + 1)


def _candidate_cells(tr):
    r = lax.broadcasted_iota(I32, (_N_CAND_ROWS, tr), 0)
    grp, i = lax.shift_right_logical(r, 3), jnp.bitwise_and(r, 7)
    last = len(_CAND_GROUPS)
    j = jnp.where(grp <= 1, 0, jnp.where(grp < last, grp - 1, 8 + i))
    k = jnp.where(grp == 1, 8 + i, jnp.where(grp == last, 0, i))
    return (j * PEER_TOPK + k).astype(F32)


def _route_kernel(x_ref, g_ref, wq_ref, k1_ref, k2_ref, h_ref, ids_ref, gates_ref,
                  q_s, work1, work2, top_s, idx_s, best_v, best_p, ids_t, gates_t):
    k = PEER_TOPK
    tr = x_ref.shape[0]
    hb = _rms(x_ref[...], g_ref[...]).astype(BF16)
    h_ref[...] = hb
    q = _dot(hb, wq_ref[...]).astype(BF16)
    for i in range(2 * PEER_HEADS):
        q_s[i] = q[:, i * PEER_HALF:(i + 1) * PEER_HALF]
    key_rows = lax.broadcasted_iota(I32, (PEER_N_KEYS, tr), 0).astype(F32)
    cand_id = _candidate_cells(tr)
    row8 = lax.broadcasted_iota(I32, (8, tr), 0)

    def head(h, c):
        for half, key_ref in enumerate((k1_ref, k2_ref)):
            work1[half] = _dot_nt(key_ref[...], q_s[2 * h + half])
        _extract_topk([(work1.at[half], key_rows, top_s.at[half], idx_s.at[half]) for half in range(2)])
        for g, (j, k0) in enumerate(_CAND_GROUPS):
            sums = top_s[0, j:j + 1, :] + top_s[1, k0:k0 + 8, :]
            n_ok = PEER_TOPK // (j + 1) - k0
            work2[8 * g:8 * g + 8, :] = sums if n_ok >= 8 else jnp.where(row8 < n_ok, sums, -jnp.inf)
        work2[_N_CAND_ROWS - 8:, :] = top_s[0, 8:16, :] + top_s[1, 0:1, :]
        _extract_topk([(work2, cand_id, best_v, best_p)])
        pos = best_p[...].astype(I32)
        pj, pk = lax.shift_right_logical(pos, 4), jnp.bitwise_and(pos, k - 1)
        a = jnp.zeros(pos.shape, F32)
        b = jnp.zeros(pos.shape, F32)
        for j in range(k):
            a = jnp.where(pj == j, idx_s[0, j:j + 1, :], a)
            b = jnp.where(pk == j, idx_s[1, j:j + 1, :], b)
        best = best_v[...]
        e = jnp.exp(best - best[0:1, :])
        r0 = pl.multiple_of(h * k, k)
        ids_t[pl.ds(r0, k), :] = a * PEER_N_KEYS + b
        gates_t[pl.ds(r0, k), :] = e / jnp.sum(e, axis=0, keepdims=True)
        return c

    lax.fori_loop(0, PEER_HEADS, head, 0)
    ids_ref[...] = ids_t[...].T.astype(I32)
    gates_ref[...] = gates_t[...].T


def _peer_route(x, g, wq, k1, k2, *, tr):
    t, d = x.shape
    full = lambda a: pl.BlockSpec(a.shape, lambda i: (0,) * a.ndim)
    row = lambda w: pl.BlockSpec((tr, w), lambda i: (i, 0))
    k = PEER_TOPK
    return pl.pallas_call(
        _route_kernel,
        grid=(t // tr,),
        in_specs=[row(d), full(g), full(wq), full(k1), full(k2)],
        out_specs=[row(d), row(LANES), row(LANES)],
        out_shape=[jax.ShapeDtypeStruct((t, d), BF16), jax.ShapeDtypeStruct((t, LANES), I32),
                   jax.ShapeDtypeStruct((t, LANES), F32)],
        scratch_shapes=[pltpu.VMEM((2 * PEER_HEADS, tr, PEER_HALF), BF16),
                        pltpu.VMEM((2, PEER_N_KEYS, tr), F32), pltpu.VMEM((_N_CAND_ROWS, tr), F32),
                        pltpu.VMEM((2, k, tr), F32), pltpu.VMEM((2, k, tr), F32),
                        pltpu.VMEM((k, tr), F32), pltpu.VMEM((k, tr), F32),
                        pltpu.VMEM((PEER_HEADS * k, tr), F32), pltpu.VMEM((PEER_HEADS * k, tr), F32)],
        compiler_params=_cparams(("arbitrary",), VMEM_MIB["peer_route"]),
        name="peer_route",
    )(x, g, wq, k1, k2)


def _gelu(x):
    return 0.5 * x * (1.0 + lax.erf(x * np.float32(np.sqrt(0.5))))


def _peer_kernel(final_norm, n_blocks, h_ref, ids_ref, gates_ref, ids_next_ref, gates_next_ref, u_ref, v_ref,
                 x_ref, fg_ref, o_ref, gate_s, a_s, acc_s):
    tq = h_ref.shape[0]
    nb = u_ref.shape[0]
    d = v_ref.shape[1]
    n_chunks = nb // PEER_CHUNK
    i, j = pl.program_id(0), pl.program_id(1)
    last_tile = pl.num_programs(0) - 1
    key = lax.broadcasted_iota(I32, (PEER_N_KEYS, LANES), 0)

    def build_gates(ids_r, gates_r, t0):
        ids = ids_r[pl.ds(t0, GATE_UNROLL), :]
        gate = gates_r[pl.ds(t0, GATE_UNROLL), :]
        i1 = lax.shift_right_logical(ids, 7)
        i2 = jnp.bitwise_and(ids, PEER_N_KEYS - 1)
        for k in range(GATE_UNROLL):
            p = jnp.where(i1[k:k + 1] == key, gate[k:k + 1], 0.0).astype(BF16)
            q = jnp.where(i2[k:k + 1] == key, 1.0, 0.0).T.astype(BF16)
            g = _dot(p, q).astype(BF16)
            row = (t0 + k) * GATE_PITCH
            row = row if isinstance(row, int) else pl.multiple_of(row, 8)
            gate_s[pl.ds(row, PEER_N_KEYS // 2), :] = pltpu.bitcast(g, jnp.uint32)

    def activate(block, dst):
        for c in range(n_chunks):
            cols = slice(c * PEER_CHUNK, (c + 1) * PEER_CHUNK)
            act = _dot_nt(h_ref[...], u_ref[cols, :])
            word = gate_s[pl.ds(block * n_chunks + c, tq, stride=GATE_PITCH), :]
            g_even = pltpu.bitcast(lax.shift_left(word, jnp.uint32(16)), F32)
            g_odd = pltpu.bitcast(jnp.bitwise_and(word, jnp.uint32(0xFFFF0000)), F32)
            a = jnp.concatenate([_gelu(act[:, :LANES]) * g_even, _gelu(act[:, LANES:]) * g_odd], axis=1)
            dst[:, cols] = a.astype(BF16)

    def mix(src):
        acc_s[...] += _dot(src[...], v_ref[...])

    @pl.when((i == 0) & (j == 0))
    def _():
        def build(grp, c):
            build_gates(ids_ref, gates_ref, pl.multiple_of(grp * GATE_UNROLL, GATE_UNROLL))
            return c

        lax.fori_loop(0, tq // GATE_UNROLL, build, 0)

    @pl.when(j == 0)
    def _():
        acc_s[...] = jnp.zeros_like(acc_s)
        activate(0, a_s.at[0])

    for parity in range(2):
        @pl.when((j >= 1) & (j < n_blocks) & (lax.rem(j, 2) == parity))
        def _():
            mix(a_s.at[1 - parity])
            activate(j, a_s.at[parity])

    def finalize():
        y = x_ref[...] + acc_s[...]
        o_ref[...] = _rms(y, fg_ref[...]) if final_norm else y

    last_src = a_s.at[(n_blocks - 1) % 2]

    @pl.when((j == n_blocks) & (i < last_tile))
    def _():
        n_parts = d // PEER_CHUNK
        for part in range(n_parts):
            cols = slice(part * PEER_CHUNK, (part + 1) * PEER_CHUNK)
            acc_s[:, cols] += _dot(last_src[...], v_ref[:, cols])
            for t0 in range(part * tq // n_parts, (part + 1) * tq // n_parts, GATE_UNROLL):
                build_gates(ids_next_ref, gates_next_ref, t0)
        finalize()

    @pl.when((j == n_blocks) & (i == last_tile))
    def _():
        mix(last_src)
        finalize()


def _peer_main(h, ids, gates, u, v, x, fg, *, final_norm, tq, nb):
    t, d = x.shape
    n_blocks = u.shape[0] // nb
    n_tiles = t // tq
    tok = lambda w: pl.BlockSpec((tq, w), lambda i, j: (i, 0))
    tok_next = pl.BlockSpec((tq, LANES), lambda i, j: (jnp.minimum(i + 1, n_tiles - 1), 0))
    return pl.pallas_call(
        functools.partial(_peer_kernel, final_norm, n_blocks),
        grid=(n_tiles, n_blocks + 1),
        in_specs=[tok(d), tok(LANES), tok(LANES), tok_next, tok_next,
                  pl.BlockSpec((nb, d), lambda i, j: (jnp.minimum(j, n_blocks - 1), 0)),
                  pl.BlockSpec((nb, d), lambda i, j: (jnp.maximum(j - 1, 0), 0)),
                  tok(d), pl.BlockSpec((1, d), lambda i, j: (0, 0))],
        out_specs=tok(d),
        out_shape=jax.ShapeDtypeStruct((t, d), F32),
        scratch_shapes=[pltpu.VMEM((tq * GATE_PITCH, LANES), jnp.uint32), pltpu.VMEM((2, tq, nb), BF16),
                        pltpu.VMEM((tq, d), F32)],
        compiler_params=_cparams(("arbitrary", "arbitrary"), VMEM_MIB["peer_main"]),
        name="peer_main",
    )(h, ids, gates, ids, gates, u, v, x, fg)


def _prep_w_in(w_in, fgate_b):
    d = w_in.shape[0]
    o = np.cumsum([0, 384, 384, 384, FOX_HEADS, MLA_Q_RANK, MLA_KV_RANK, MLA_ROPE_DIM, 384, 384, 384])
    fq, fk, fv, fl, cq, ckv, kr, sq, sk, sv = [w_in[:, o[i]:o[i + 1]] for i in range(10)]
    half = MLA_ROPE_DIM // 2
    kr4 = jnp.concatenate([kr[:, :half]] * MLA_HEADS + [kr[:, half:]] * MLA_HEADS, axis=1)
    fl_pad = jnp.pad(fl, ((0, 0), (0, LANES - FOX_HEADS)))
    w_all = jnp.concatenate([fq, fk, fv, sq, sk, sv, cq, ckv, fl_pad, kr4], axis=1).astype(BF16)
    assert w_all.shape == (d, _IN_COLS_PADDED)
    fb = jnp.pad(fgate_b, (0, LANES - FOX_HEADS)).reshape(1, LANES)
    return w_all, fb


def _prep_w_uq(w_uq):
    per = MLA_NOPE_DIM + MLA_ROPE_DIM
    half = MLA_ROPE_DIM // 2
    nope = [w_uq[:, h * per:h * per + MLA_NOPE_DIM] for h in range(MLA_HEADS)]
    x1 = [w_uq[:, h * per + MLA_NOPE_DIM:h * per + MLA_NOPE_DIM + half] for h in range(MLA_HEADS)]
    x2 = [w_uq[:, h * per + MLA_NOPE_DIM + half:(h + 1) * per] for h in range(MLA_HEADS)]
    return jnp.concatenate(nope + x1 + x2, axis=1).astype(BF16)


def _prep_w_ukv(w_ukv):
    per = MLA_NOPE_DIM + MLA_V_DIM
    kn = [w_ukv[:, h * per:h * per + MLA_NOPE_DIM] for h in range(MLA_HEADS)]
    vv = [w_ukv[:, h * per + MLA_NOPE_DIM:(h + 1) * per] for h in range(MLA_HEADS)]
    return jnp.concatenate(kn + vv, axis=1).astype(BF16)


def _rope_lane_freq():
    half = MLA_ROPE_DIM // 2
    inv_freq = 1.0 / (ROPE_THETA ** (jnp.arange(half, dtype=F32) / half))
    return jnp.tile(inv_freq, LANES // half).reshape(1, LANES)


def kernel(x, attn_norm_g, w_in, fgate_b, mla_q_norm_g, w_uq, mla_kv_norm_g, w_ukv, out_norm_fox, out_norm_mla, out_norm_sb, w_out, ffn_norm_g, peer_w_query, peer_keys1, peer_keys2, peer_u, peer_v, final_norm_g):
    b, s, d = x.shape
    depth = w_in.shape[0]
    t = b * s
    xt = x.reshape(t, d)
    freq = _rope_lane_freq()
    row = lambda a: a.reshape(1, -1)
    for l in range(depth):
        w_all, fb = _prep_w_in(w_in[l], fgate_b[l])
        fq, fk, fv, sq, sk, sv, logf, mq, mk, mv = _inproj(
            xt, row(attn_norm_g[l]), w_all, fb, row(mla_q_norm_g[l]), _prep_w_uq(w_uq[l]),
            row(mla_kv_norm_g[l]), _prep_w_ukv(w_ukv[l]), freq, seq=s, ts=TOKEN_TILE)
        bs = lambda a: a.reshape(b, s, a.shape[1])
        c = _fox_cumsum(bs(logf))
        y_fox = _pair_attention("fox", bs(fq), bs(fk), bs(fv), c, tq=TOKEN_TILE, bk=KEY_BLOCK["fox"])
        y_mla = _pair_attention("mla", bs(mq), bs(mk), bs(mv), tq=TOKEN_TILE, bk=KEY_BLOCK["mla"])
        y_sb = _pair_attention("sb", bs(sq), bs(sk), bs(sv), tq=TOKEN_TILE, bk=KEY_BLOCK["sb"])
        wo = w_out[l].astype(BF16)
        xt = _outproj(xt, y_fox.reshape(t, -1), y_mla.reshape(t, -1), y_sb.reshape(t, -1),
                      row(out_norm_fox[l]), row(out_norm_mla[l]), row(out_norm_sb[l]),
                      wo[:_W384], wo[_W384:_W384 + _W_MLA], wo[_W384 + _W_MLA:], ts=TOKEN_TILE)
        h2, ids, gates = _peer_route(xt, row(ffn_norm_g[l]), peer_w_query[l].astype(BF16),
                                     peer_keys1[l].astype(BF16), peer_keys2[l].astype(BF16), tr=ROUTE_TILE)
        xt = _peer_main(h2, ids, gates, peer_u[l].astype(BF16), peer_v[l].astype(BF16), xt,
                        row(final_norm_g), final_norm=(l == depth - 1), tq=TOKEN_TILE, nb=EXPERT_BLOCK)
    return xt.reshape(b, s, d)
```
